```python
import math
import jax, jax.numpy as jnp
from jax import lax
import numpy as np

D_MODEL = 1024
BATCH = 8
SEQ = 2048
DEPTH = 4
DEC_BATCH = 128
DEC_SEQ = 1
PAST_LEN = 16384
PAGE_SIZE = 128

N_AB = (DEPTH + 1) // 2
N_C = DEPTH // 2
D_A = D_MODEL // 2
CONV_W = 31
D_B = D_MODEL // 2
S5_GH = 16
S5_G = D_B // S5_GH
S5_P = 64
HG_K = 128
HG_V = 128
HG_HEADS = D_MODEL // HG_K
D_C = HG_HEADS * HG_K
HG_CHUNK = 64
D_FF = 2752
N_EXP = 8
TOP_K = 2
D_EXP = 3584
EPS = 1e-6
LN_EPS = 1e-5
FORGET_FLOOR = 1e-30
F32 = jnp.float32

kernel_name = "hybrid_conv_s5_hgrn2_moe_step"


def _rmsnorm(x, g):
    xf = x.astype(F32)
    y = xf * lax.rsqrt(jnp.mean(xf * xf, axis=-1, keepdims=True) + EPS)
    return (y * g.astype(F32)).astype(x.dtype)


def _swiglu(x, wg, wu, wd):
    return (jax.nn.silu(x @ wg) * (x @ wu)) @ wd


def _conv_module(a_val, a_gate, buf, w, b, ln_g, ln_b):
    u = a_val * jax.nn.sigmoid(a_gate)
    full = jnp.concatenate([buf.astype(u.dtype), u], axis=1)
    new_buf = full[:, -(CONV_W - 1):]
    y = lax.conv_general_dilated(full, w.astype(u.dtype)[:, None, :], window_strides=(1,),
                                 padding='VALID', dimension_numbers=('NWC', 'WIO', 'NWC'),
                                 feature_group_count=D_A) + b.astype(u.dtype)
    yf = y.astype(F32)
    mu = jnp.mean(yf, axis=-1, keepdims=True)
    var = jnp.mean(jnp.square(yf - mu), axis=-1, keepdims=True)
    yf = (yf - mu) * lax.rsqrt(var + LN_EPS) * ln_g.astype(F32) + ln_b.astype(F32)
    return jax.nn.silu(yf).astype(u.dtype), new_buf


def _s5(u, h0_re, h0_im, a_re, a_im, log_dt, b_re, b_im, c_re, c_im, d, glu_w, glu_b):
    N, T, _ = u.shape
    uf = u.astype(F32).reshape(N, T, S5_G, S5_GH)
    dt = jnp.exp(log_dt.astype(F32))[:, None]
    ar, ai = a_re.astype(F32), a_im.astype(F32)
    mag = jnp.exp(dt * ar)
    ang = dt * ai
    abar_re, abar_im = mag * jnp.cos(ang), mag * jnp.sin(ang)
    den = ar * ar + ai * ai
    nr, ni = abar_re - 1.0, abar_im
    coef_re = (nr * ar + ni * ai) / den
    coef_im = (ni * ar - nr * ai) / den
    br, bi = b_re.astype(F32), b_im.astype(F32)
    bbar_re = coef_re[..., None] * br - coef_im[..., None] * bi
    bbar_im = coef_re[..., None] * bi + coef_im[..., None] * br
    bu_re = jnp.einsum('gph,btgh->btgp', bbar_re, uf)
    bu_im = jnp.einsum('gph,btgh->btgp', bbar_im, uf)
    a_seq_re = jnp.broadcast_to(abar_re, (1, T, S5_G, S5_P))
    a_seq_im = jnp.broadcast_to(abar_im, (1, T, S5_G, S5_P))

    def combine(e1, e2):
        a1r, a1i, b1r, b1i = e1
        a2r, a2i, b2r, b2i = e2
        return (a1r * a2r - a1i * a2i, a1r * a2i + a1i * a2r,
                a2r * b1r - a2i * b1i + b2r, a2r * b1i + a2i * b1r + b2i)

    acr, aci, hr, hi = lax.associative_scan(combine, (a_seq_re, a_seq_im, bu_re, bu_im), axis=1)
    h0r = h0_re.astype(F32)[:, None]
    h0i = h0_im.astype(F32)[:, None]
    hr = hr + acr * h0r - aci * h0i
    hi = hi + acr * h0i + aci * h0r
    y = (jnp.einsum('ghp,btgp->btgh', c_re.astype(F32), hr)
         - jnp.einsum('ghp,btgp->btgh', c_im.astype(F32), hi))
    y = (y + d.astype(F32).reshape(S5_G, S5_GH) * uf).reshape(N, T, D_B)
    y = jax.nn.gelu(y, approximate=False)
    y = y * jax.nn.sigmoid(y @ glu_w.astype(F32) + glu_b.astype(F32))
    return y.astype(u.dtype), hr[:, -1], hi[:, -1]


def _gated_chunk_recurrence(q, k, v, log_g, s0):
    N, T, H, K = q.shape
    V = v.shape[-1]
    L = min(HG_CHUNK, T)
    n_chunks = -(-T // L)
    pad = n_chunks * L - T

    def prep(a):
        a = jnp.pad(a, ((0, 0), (0, pad), (0, 0), (0, 0)))
        return a.reshape(N, n_chunks, L, H, a.shape[-1]).transpose(1, 0, 3, 2, 4)

    mask = jnp.tril(jnp.ones((L, L), dtype=bool))[:, :, None]

    def step(S, inp):
        qc, kc, vc, gc = inp
        G = jnp.cumsum(gc, axis=2)
        diff = G[:, :, :, None, :] - G[:, :, None, :, :]
        decay = jnp.where(mask, jnp.exp(jnp.where(mask, diff, 0.0)), 0.0)
        scores = jnp.einsum('bhtk,bhsk,bhtsk->bhts', qc, kc, decay)
        o = (jnp.einsum('bhts,bhsv->bhtv', scores, vc)
             + jnp.einsum('bhtk,bhkv->bhtv', qc * jnp.exp(G), S))
        GL = G[:, :, -1:, :]
        S = (jnp.exp(GL[:, :, 0, :])[..., None] * S
             + jnp.einsum('bhsk,bhsv->bhkv', kc * jnp.exp(GL - G), vc))
        return S, o

    S, o = lax.scan(step, s0, (prep(q), prep(k), prep(v), prep(log_g)))
    o = o.transpose(1, 0, 3, 2, 4).reshape(N, n_chunks * L, H, V)[:, :T]
    return o, S


def _hgrn2(x, s0, w_in, lb, gnorm, w_out):
    N, T, _ = x.shape
    proj = x @ w_in
    q, f, i, g = jnp.split(proj, 4, axis=-1)
    qf = jax.nn.silu(q.astype(F32)).reshape(N, T, HG_HEADS, HG_K)
    ff = f.astype(F32)
    lbf = lb.astype(F32)
    forget = lbf + (1.0 - lbf) * jax.nn.sigmoid(ff)
    log_g = jnp.log(jnp.maximum(forget, FORGET_FLOOR))
    kf = (1.0 - lbf) * jax.nn.sigmoid(-ff)
    o, s_new = _gated_chunk_recurrence(qf, kf.reshape(N, T, HG_HEADS, HG_K),
                                       i.astype(F32).reshape(N, T, HG_HEADS, HG_V),
                                       log_g.reshape(N, T, HG_HEADS, HG_K), s0.astype(F32))
    o = o * lax.rsqrt(jnp.mean(o * o, axis=-1, keepdims=True) + EPS) * gnorm.astype(F32)
    o = o.reshape(N, T, D_C) * jax.nn.silu(g.astype(F32))
    return o.astype(x.dtype) @ w_out, s_new


def _moe(x, router, wg, wu, wd):
    logits = (x @ router).astype(F32)
    top_v, top_i = lax.top_k(logits, TOP_K)
    gates = jax.nn.softmax(top_v, axis=-1)
    out = jnp.zeros(x.shape, F32)
    for e in range(N_EXP):
        ge = jnp.sum(jnp.where(top_i == e, gates, 0.0), axis=-1)
        out = out + ge[..., None] * _swiglu(x, wg[e], wu[e], wd[e]).astype(F32)
    return out.astype(x.dtype)


def _trunk(x, conv_buf, ssm_re, ssm_im, hg_state, p):
    h = x
    new_conv, new_re, new_im, new_hg = [], [], [], []
    lbs = jax.nn.softmax(p['hg_lb_logits'].astype(F32), axis=0)
    lbs = jnp.cumsum(lbs, axis=0) - lbs[0:1]
    for l in range(DEPTH):
        j = l // 2
        xn = _rmsnorm(h, p['norm_mix'][l])
        if l % 2 == 0:
            proj = xn @ p['ab_w_in'][j]
            a_val, a_gate, u = proj[..., :D_A], proj[..., D_A:2 * D_A], proj[..., 2 * D_A:]
            ya, cb = _conv_module(a_val, a_gate, conv_buf[j], p['conv_w'][j], p['conv_b'][j],
                                  p['conv_ln_g'][j], p['conv_ln_b'][j])
            yb, hr, hi = _s5(u, ssm_re[j], ssm_im[j], p['ssm_a_re'][j], p['ssm_a_im'][j],
                             p['ssm_log_dt'][j], p['ssm_b_re'][j], p['ssm_b_im'][j],
                             p['ssm_c_re'][j], p['ssm_c_im'][j], p['ssm_d'][j],
                             p['ssm_glu_w'][j], p['ssm_glu_b'][j])
            h = h + jnp.concatenate([ya, yb], axis=-1) @ p['ab_w_out'][j]
            h = h + _swiglu(_rmsnorm(h, p['norm_ffn'][l]), p['ffn_w_gate'][j],
                            p['ffn_w_up'][j], p['ffn_w_down'][j])
            new_conv.append(cb)
            new_re.append(hr)
            new_im.append(hi)
        else:
            y, s = _hgrn2(xn, hg_state[j], p['hg_w_in'][j], lbs[j], p['hg_gnorm'][j],
                          p['hg_w_out'][j])
            h = h + y
            h = h + _moe(_rmsnorm(h, p['norm_ffn'][l]), p['moe_router'][j], p['moe_w_gate'][j],
                         p['moe_w_up'][j], p['moe_w_down'][j])
            new_hg.append(s)
    y = _rmsnorm(h, p['norm_final'])
    return y, jnp.stack(new_conv), jnp.stack(new_re), jnp.stack(new_im), jnp.stack(new_hg)


def setup_inputs(seed: int = 0) -> dict:
    key = jax.random.key(seed)
    ks = iter(jax.random.split(key, 48))

    def nrm(shape, scale):
        return jax.random.normal(next(ks), shape, F32) * scale

    n_idx = jnp.arange(S5_P, dtype=F32)
    return {
        'x_prompt': nrm((BATCH, SEQ, D_MODEL), 1.0),
        'x_sample': nrm((DEC_BATCH, DEC_SEQ, D_MODEL), 1.0),
        'state_conv': nrm((N_AB, DEC_BATCH, CONV_W - 1, D_A), 0.5),
        'state_ssm_re': nrm((N_AB, DEC_BATCH, S5_G, S5_P), 0.5),
        'state_ssm_im': nrm((N_AB, DEC_BATCH, S5_G, S5_P), 0.5),
        'state_hgrn': nrm((N_C, DEC_BATCH, HG_HEADS, HG_K, HG_V), 1.0),
        'norm_mix': 1.0 + nrm((DEPTH, D_MODEL), 0.01),
        'norm_ffn': 1.0 + nrm((DEPTH, D_MODEL), 0.01),
        'norm_final': 1.0 + nrm((D_MODEL,), 0.01),
        'ab_w_in': nrm((N_AB, D_MODEL, 2 * D_A + D_B), D_MODEL ** -0.5),
        'ab_w_out': nrm((N_AB, D_A + D_B, D_MODEL), (D_A + D_B) ** -0.5),
        'conv_w': nrm((N_AB, CONV_W, D_A), CONV_W ** -0.5),
        'conv_b': nrm((N_AB, D_A), 0.01),
        'conv_ln_g': 1.0 + nrm((N_AB, D_A), 0.01),
        'conv_ln_b': nrm((N_AB, D_A), 0.01),
        'ssm_a_re': -0.5 + nrm((N_AB, S5_G, S5_P), 0.01),
        'ssm_a_im': math.pi * n_idx + nrm((N_AB, S5_G, S5_P), 0.01),
        'ssm_log_dt': jax.random.uniform(next(ks), (N_AB, S5_G), F32,
                                         math.log(1e-3), math.log(1e-1)),
        'ssm_b_re': nrm((N_AB, S5_G, S5_P, S5_GH), (2 * S5_GH) ** -0.5),
        'ssm_b_im': nrm((N_AB, S5_G, S5_P, S5_GH), (2 * S5_GH) ** -0.5),
        'ssm_c_re': nrm((N_AB, S5_G, S5_GH, S5_P), S5_P ** -0.5),
        'ssm_c_im': nrm((N_AB, S5_G, S5_GH, S5_P), S5_P ** -0.5),
        'ssm_d': nrm((N_AB, D_B), 1.0),
        'ssm_glu_w': nrm((N_AB, D_B, D_B), D_B ** -0.5),
        'ssm_glu_b': nrm((N_AB, D_B), 0.01),
        'hg_w_in': nrm((N_C, D_MODEL, 4 * D_C), D_MODEL ** -0.5),
        'hg_lb_logits': nrm((N_C, D_C), 0.1),
        'hg_gnorm': 1.0 + nrm((N_C, HG_V), 0.01),
        'hg_w_out': nrm((N_C, D_C, D_MODEL), D_C ** -0.5),
        'ffn_w_gate': nrm((N_AB, D_MODEL, D_FF), D_MODEL ** -0.5),
        'ffn_w_up': nrm((N_AB, D_MODEL, D_FF), D_MODEL ** -0.5),
        'ffn_w_down': nrm((N_AB, D_FF, D_MODEL), D_FF ** -0.5),
        'moe_router': nrm((N_C, D_MODEL, N_EXP), D_MODEL ** -0.5),
        'moe_w_gate': nrm((N_C, N_EXP, D_MODEL, D_EXP), D_MODEL ** -0.5),
        'moe_w_up': nrm((N_C, N_EXP, D_MODEL, D_EXP), D_MODEL ** -0.5),
        'moe_w_down': nrm((N_C, N_EXP, D_EXP, D_MODEL), D_EXP ** -0.5),
    }


def reference(x_prompt, x_sample, state_conv, state_ssm_re, state_ssm_im, state_hgrn,
              norm_mix, norm_ffn, norm_final, ab_w_in, ab_w_out, conv_w, conv_b, conv_ln_g,
              conv_ln_b, ssm_a_re, ssm_a_im, ssm_log_dt, ssm_b_re, ssm_b_im, ssm_c_re, ssm_c_im,
              ssm_d, ssm_glu_w, ssm_glu_b, hg_w_in, hg_lb_logits, hg_gnorm, hg_w_out,
              ffn_w_gate, ffn_w_up, ffn_w_down, moe_router, moe_w_gate, moe_w_up, moe_w_down):
    p = dict(norm_mix=norm_mix, norm_ffn=norm_ffn, norm_final=norm_final, ab_w_in=ab_w_in,
             ab_w_out=ab_w_out, conv_w=conv_w, conv_b=conv_b, conv_ln_g=conv_ln_g,
             conv_ln_b=conv_ln_b, ssm_a_re=ssm_a_re, ssm_a_im=ssm_a_im, ssm_log_dt=ssm_log_dt,
             ssm_b_re=ssm_b_re, ssm_b_im=ssm_b_im, ssm_c_re=ssm_c_re, ssm_c_im=ssm_c_im,
             ssm_d=ssm_d, ssm_glu_w=ssm_glu_w, ssm_glu_b=ssm_glu_b, hg_w_in=hg_w_in,
             hg_lb_logits=hg_lb_logits, hg_gnorm=hg_gnorm, hg_w_out=hg_w_out,
             ffn_w_gate=ffn_w_gate, ffn_w_up=ffn_w_up, ffn_w_down=ffn_w_down,
             moe_router=moe_router, moe_w_gate=moe_w_gate, moe_w_up=moe_w_up,
             moe_w_down=moe_w_down)
    zc = jnp.zeros((N_AB, BATCH, CONV_W - 1, D_A), x_prompt.dtype)
    zs = jnp.zeros((N_AB, BATCH, S5_G, S5_P), F32)
    zh = jnp.zeros((N_C, BATCH, HG_HEADS, HG_K, HG_V), F32)
    y_prompt, conv_p, ssm_re_p, ssm_im_p, hgrn_p = _trunk(x_prompt, zc, zs, zs, zh, p)
    y_sample, conv_s, ssm_re_s, ssm_im_s, hgrn_s = _trunk(
        x_sample, state_conv, state_ssm_re, state_ssm_im, state_hgrn, p)
    return (y_prompt, y_sample, conv_p, ssm_re_p, ssm_im_p, hgrn_p,
            conv_s, ssm_re_s, ssm_im_s, hgrn_s)
```

```python
import functools
import math

import jax
import jax.numpy as jnp
from jax import lax
from jax.experimental import pallas as pl
from jax.experimental.pallas import tpu as pltpu

F32 = jnp.float32
BF16 = jnp.bfloat16
I32 = jnp.int32

EPS = 1e-6
LN_EPS = 1e-5
FORGET_FLOOR = 1e-30

D_MODEL = 1024
D_A = 512
D_B = 512
CONV_W = 31
CONV_HALO = 32
S5_G = 32
S5_P = 64
S5_GH = 16
S5_STATE = S5_G * S5_P
S5_CHUNKS = 4
HG_HEADS = 8
HG_K = 128
HG_V = 128
HG_CHUNK = 64
D_FF = 2752
D_FF_PAD = 2816
N_EXP = 8
D_EXP = 3584
LANES = 128
VMEM_LIMIT_BYTES = 56 * 1024 * 1024
EXP_CLAMP = 80.0

MOE_TM = 512
MOE_TK = 512
DISPATCH_TR = 256
DISPATCH_TC = 384
COMBINE_W = 128
COMBINE_SLOTS = 12


def _cparams(*sem):
    return pltpu.CompilerParams(dimension_semantics=sem, vmem_limit_bytes=VMEM_LIMIT_BYTES)


def _sigmoid(x):
    return 1.0 / (1.0 + jnp.exp(-x))


def _silu(x):
    return x * _sigmoid(x)


def _rms(x, g):
    return x * lax.rsqrt(jnp.mean(x * x, axis=-1, keepdims=True) + EPS) * g


def _bdot(a, b):
    return jnp.dot(a.astype(BF16), b.astype(BF16), preferred_element_type=F32)


def _norm_matmul_kernel(x_ref, g_ref, w_ref, o_ref, xn_ref):
    @pl.when(pl.program_id(1) == 0)
    def _():
        xn_ref[...] = _rms(x_ref[...], g_ref[...]).astype(BF16)

    o_ref[...] = jnp.dot(xn_ref[...], w_ref[...], preferred_element_type=F32).astype(o_ref.dtype)


def _norm_matmul(x, g, w, tn, out_dtype=F32):
    n, d = x.shape
    nout = w.shape[1]
    tm = min(512, n)
    return pl.pallas_call(
        _norm_matmul_kernel,
        grid=(n // tm, nout // tn),
        in_specs=[
            pl.BlockSpec((tm, d), lambda i, j: (i, 0)),
            pl.BlockSpec((1, d), lambda i, j: (0, 0)),
            pl.BlockSpec((d, tn), lambda i, j: (0, j)),
        ],
        out_specs=pl.BlockSpec((tm, tn), lambda i, j: (i, j)),
        out_shape=jax.ShapeDtypeStruct((n, nout), out_dtype),
        scratch_shapes=[pltpu.VMEM((tm, d), BF16)],
        compiler_params=_cparams("parallel", "arbitrary"),
        name="norm_matmul",
    )(x, g, w)


def _conv_post(y, b, lg, lb):
    y = y + b
    mu = jnp.mean(y, axis=-1, keepdims=True)
    yc = y - mu
    var = jnp.mean(yc * yc, axis=-1, keepdims=True)
    return _silu(yc * lax.rsqrt(var + LN_EPS) * lg + lb)


def _conv_prompt_kernel(av_ref, ag_ref, w_ref, b_ref, lg_ref, lb_ref, y_ref, nb_ref, ubuf, *, tt, rb):
    t = pl.program_id(1)

    @pl.when(t == 0)
    def _():
        ubuf[0:CONV_HALO, :] = jnp.zeros((CONV_HALO, D_A), F32)

    @pl.when(t > 0)
    def _():
        ubuf[0:CONV_HALO, :] = ubuf[tt:tt + CONV_HALO, :]

    ubuf[CONV_HALO:CONV_HALO + tt, :] = av_ref[0] * _sigmoid(ag_ref[0])
    off = CONV_HALO - (CONV_W - 1)
    for r in range(tt // rb):
        acc = jnp.zeros((rb, D_A), F32)
        for k in range(CONV_W):
            acc = acc + w_ref[k:k + 1, :] * ubuf[r * rb + k + off:r * rb + k + off + rb, :]
        y_ref[0, r * rb:(r + 1) * rb, :] = _conv_post(acc, b_ref[...], lg_ref[...], lb_ref[...]).astype(y_ref.dtype)

    @pl.when(t == pl.num_programs(1) - 1)
    def _():
        nb_ref[0] = ubuf[tt + off:tt + CONV_HALO, :]


def _conv_prompt(proj3, w, b, lg, lb):
    bsz, t, _ = proj3.shape
    tt = min(256, t)
    kern = functools.partial(_conv_prompt_kernel, tt=tt, rb=32)
    vec = pl.BlockSpec((1, D_A), lambda i, j: (0, 0))
    return pl.pallas_call(
        kern,
        grid=(bsz, t // tt),
        in_specs=[
            pl.BlockSpec((1, tt, D_A), lambda i, j: (i, j, 0)),
            pl.BlockSpec((1, tt, D_A), lambda i, j: (i, j, 1)),
            pl.BlockSpec((CONV_W, D_A), lambda i, j: (0, 0)),
            vec, vec, vec,
        ],
        out_specs=[
            pl.BlockSpec((1, tt, D_A), lambda i, j: (i, j, 0)),
            pl.BlockSpec((1, CONV_W - 1, D_A), lambda i, j: (i, 0, 0)),
        ],
        out_shape=[
            jax.ShapeDtypeStruct((bsz, t, D_A), BF16),
            jax.ShapeDtypeStruct((bsz, CONV_W - 1, D_A), F32),
        ],
        scratch_shapes=[pltpu.VMEM((CONV_HALO + tt, D_A), F32)],
        compiler_params=_cparams("parallel", "arbitrary"),
        name="conv_prompt",
    )(proj3, proj3, w, b, lg, lb)


def _conv_sample_kernel(av_ref, ag_ref, buf_ref, w_ref, b_ref, lg_ref, lb_ref, y_ref, nb_ref):
    u = av_ref[...] * _sigmoid(ag_ref[...])
    acc = w_ref[CONV_W - 1:CONV_W, :] * u
    for k in range(CONV_W - 1):
        acc = acc + w_ref[k:k + 1, :] * buf_ref[k]
    y_ref[...] = _conv_post(acc, b_ref[...], lg_ref[...], lb_ref[...]).astype(y_ref.dtype)
    for k in range(CONV_W - 2):
        nb_ref[k] = buf_ref[k + 1]
    nb_ref[CONV_W - 2] = u


def _conv_sample(proj, buf_t, w, b, lg, lb):
    n = proj.shape[0]
    nbk = min(32, n)
    vec = pl.BlockSpec((1, D_A), lambda i: (0, 0))
    return pl.pallas_call(
        _conv_sample_kernel,
        grid=(n // nbk,),
        in_specs=[
            pl.BlockSpec((nbk, D_A), lambda i: (i, 0)),
            pl.BlockSpec((nbk, D_A), lambda i: (i, 1)),
            pl.BlockSpec((CONV_W - 1, nbk, D_A), lambda i: (0, i, 0)),
            pl.BlockSpec((CONV_W, D_A), lambda i: (0, 0)),
            vec, vec, vec,
        ],
        out_specs=[
            pl.BlockSpec((nbk, D_A), lambda i: (i, 0)),
            pl.BlockSpec((CONV_W - 1, nbk, D_A), lambda i: (0, i, 0)),
        ],
        out_shape=[
            jax.ShapeDtypeStruct((n, D_A), BF16),
            jax.ShapeDtypeStruct((CONV_W - 1, n, D_A), F32),
        ],
        compiler_params=_cparams("parallel"),
        name="conv_sample",
    )(proj, proj, buf_t, w, b, lg, lb)


def _s5_kernel(u_ref, h0r_ref, h0i_ref, ar_ref, ai_ref, bcat_ref, ccat_ref, d_ref, gw_ref, gb_ref,
               y_ref, hr_out, hi_out, sre, sim, cre, cim, *, tt, nb, lw):
    i = pl.program_id(0)
    m = tt * nb
    cw = S5_STATE // S5_CHUNKS

    @pl.when(i == 0)
    def _():
        cre[...] = h0r_ref[...]
        cim[...] = h0i_ref[...]

    u = u_ref[...].reshape(m, D_B)
    ub = u.astype(BF16)
    for c in range(S5_CHUNKS):
        bu = jnp.dot(ub[:, c * LANES:(c + 1) * LANES], bcat_ref[c], preferred_element_type=F32)
        sre[:, :, c * cw:(c + 1) * cw] = bu[:, :cw].reshape(tt, nb, cw)
        sim[:, :, c * cw:(c + 1) * cw] = bu[:, cw:].reshape(tt, nb, cw)

    for c in range(S5_STATE // lw):
        ls = slice(c * lw, (c + 1) * lw)
        ar = jnp.broadcast_to(ar_ref[:, ls], (nb, lw))
        ai = jnp.broadcast_to(ai_ref[:, ls], (nb, lw))

        def body(t, carry, ls=ls, ar=ar, ai=ai):
            hr, hi = carry
            nr = ar * hr - ai * hi + sre[t, :, ls]
            ni = ar * hi + ai * hr + sim[t, :, ls]
            sre[t, :, ls] = nr
            sim[t, :, ls] = ni
            return nr, ni

        hr, hi = lax.fori_loop(0, tt, body, (cre[:, ls], cim[:, ls]), unroll=min(tt, 8))
        cre[:, ls] = hr
        cim[:, ls] = hi

    hre = sre[...].reshape(m, S5_STATE)
    him = sim[...].reshape(m, S5_STATE)
    ys = []
    for c in range(S5_CHUNKS):
        hcat = jnp.concatenate([hre[:, c * cw:(c + 1) * cw], him[:, c * cw:(c + 1) * cw]], axis=1)
        ys.append(jnp.dot(hcat.astype(BF16), ccat_ref[c], preferred_element_type=F32))
    y = jnp.concatenate(ys, axis=1) + d_ref[...] * u
    y = 0.5 * y * (1.0 + lax.erf(y * (1.0 / math.sqrt(2.0))))
    z = jnp.dot(y.astype(BF16), gw_ref[...], preferred_element_type=F32) + gb_ref[...]
    y_ref[...] = (y * _sigmoid(z)).reshape(tt, nb, D_B).astype(y_ref.dtype)

    @pl.when(i == pl.num_programs(0) - 1)
    def _():
        hr_out[...] = cre[...]
        hi_out[...] = cim[...]


def _s5(u_t, h0r, h0i, abar_re, abar_im, bcat, ccat, d, glu_w, glu_b, tt):
    t, nb, _ = u_t.shape
    lw = max(LANES, min(512, 8 * 1024 // nb))
    kern = functools.partial(_s5_kernel, tt=tt, nb=nb, lw=lw)
    full = lambda shape: pl.BlockSpec(shape, lambda i: (0,) * len(shape))
    return pl.pallas_call(
        kern,
        grid=(t // tt,),
        in_specs=[
            pl.BlockSpec((tt, nb, D_B), lambda i: (i, 0, 0)),
            full((nb, S5_STATE)), full((nb, S5_STATE)),
            full((1, S5_STATE)), full((1, S5_STATE)),
            full(bcat.shape), full(ccat.shape),
            full((1, D_B)), full((D_B, D_B)), full((1, D_B)),
        ],
        out_specs=[
            pl.BlockSpec((tt, nb, D_B), lambda i: (i, 0, 0)),
            full((nb, S5_STATE)), full((nb, S5_STATE)),
        ],
        out_shape=[
            jax.ShapeDtypeStruct((t, nb, D_B), BF16),
            jax.ShapeDtypeStruct((nb, S5_STATE), F32),
            jax.ShapeDtypeStruct((nb, S5_STATE), F32),
        ],
        scratch_shapes=[
            pltpu.VMEM((tt, nb, S5_STATE), F32), pltpu.VMEM((tt, nb, S5_STATE), F32),
            pltpu.VMEM((nb, S5_STATE), F32), pltpu.VMEM((nb, S5_STATE), F32),
        ],
        compiler_params=_cparams("arbitrary"),
        name="s5",
    )(u_t, h0r, h0i, abar_re, abar_im, bcat, ccat, d, glu_w, glu_b)


def _s5_params(a_re, a_im, log_dt, b_re, b_im, c_re, c_im):
    dt = jnp.exp(log_dt)[:, None]
    mag = jnp.exp(dt * a_re)
    ang = dt * a_im
    abar_re, abar_im = mag * jnp.cos(ang), mag * jnp.sin(ang)
    den = a_re * a_re + a_im * a_im
    nr, ni = abar_re - 1.0, abar_im
    coef_re = (nr * a_re + ni * a_im) / den
    coef_im = (ni * a_re - nr * a_im) / den
    bbar_re = coef_re[..., None] * b_re - coef_im[..., None] * b_im
    bbar_im = coef_re[..., None] * b_im + coef_im[..., None] * b_re
    gpc = S5_G // S5_CHUNKS
    eye = jnp.eye(gpc, dtype=F32)

    def bblk(x):
        x = x.reshape(S5_CHUNKS, gpc, S5_P, S5_GH)
        return jnp.einsum("cgph,gk->cghkp", x, eye).reshape(S5_CHUNKS, gpc * S5_GH, gpc * S5_P)

    def cblk(x):
        x = x.reshape(S5_CHUNKS, gpc, S5_GH, S5_P)
        return jnp.einsum("cghp,gk->cgpkh", x, eye).reshape(S5_CHUNKS, gpc * S5_P, gpc * S5_GH)

    bcat = jnp.concatenate([bblk(bbar_re), bblk(bbar_im)], axis=2).astype(BF16)
    ccat = jnp.concatenate([cblk(c_re), -cblk(c_im)], axis=1).astype(BF16)
    return abar_re.reshape(1, S5_STATE), abar_im.reshape(1, S5_STATE), bcat, ccat


def _hg_gates(q, f, lb):
    qf = _silu(q)
    forget = lb + (1.0 - lb) * _sigmoid(f)
    logg = jnp.log(jnp.maximum(forget, FORGET_FLOOR))
    kf = (1.0 - lb) * _sigmoid(-f)
    return qf, logg, kf


def _hg_out(o, g, gn):
    return o * lax.rsqrt(jnp.mean(o * o, axis=-1, keepdims=True) + EPS) * gn * _silu(g)


def _cumsum_rows(x, tri):
    hi = x.astype(BF16)
    r1 = x - hi.astype(F32)
    mid = r1.astype(BF16)
    lo = (r1 - mid.astype(F32)).astype(BF16)
    dot = lambda p: jnp.dot(tri, p, preferred_element_type=F32)
    return dot(hi) + dot(mid) + dot(lo)


def _hgrn_prompt_kernel(q_ref, f_ref, v_ref, g_ref, lb_ref, gn_ref, o_ref, s_out, st, *, tt):
    t = pl.program_id(1)
    L = HG_CHUNK

    @pl.when(t == 0)
    def _():
        st[...] = jnp.zeros(st.shape, F32)

    row = lax.broadcasted_iota(I32, (L, L), 0)
    col = lax.broadcasted_iota(I32, (L, L), 1)
    causal = row >= col
    tri = causal.astype(BF16)
    for h in range(HG_HEADS):
        hs = slice(h * HG_K, (h + 1) * HG_K)
        lb = lb_ref[:, hs]
        for c in range(tt // L):
            rs = slice(c * L, (c + 1) * L)
            q, logg, k = _hg_gates(q_ref[0, rs, hs], f_ref[0, rs, hs], lb)
            v = v_ref[0, rs, hs]
            gc = _cumsum_rows(logg, tri)
            gmid = gc[L // 2 - 1:L // 2, :]
            glast = gc[L - 1:L, :]
            qe = q * jnp.exp(jnp.minimum(gc - gmid, EXP_CLAMP))
            ke = k * jnp.exp(jnp.minimum(gmid - gc, EXP_CLAMP))
            sc = lax.dot_general(qe.astype(BF16), ke.astype(BF16), (((1,), (1,)), ((), ())),
                                 preferred_element_type=F32)
            sc = jnp.where(causal, sc, 0.0)
            s_t = st[h]
            o = _bdot(sc, v) + lax.dot_general((q * jnp.exp(gc)).astype(BF16), s_t.astype(BF16),
                                               (((1,), (1,)), ((), ())), preferred_element_type=F32)
            kd = k * jnp.exp(glast - gc)
            st[h] = jnp.exp(glast) * s_t + lax.dot_general(v.astype(BF16), kd.astype(BF16),
                                                           (((0,), (0,)), ((), ())),
                                                           preferred_element_type=F32)
            o_ref[0, rs, hs] = _hg_out(o, g_ref[0, rs, hs], gn_ref[...]).astype(o_ref.dtype)

    @pl.when(t == pl.num_programs(1) - 1)
    def _():
        for h in range(HG_HEADS):
            s_out[0, h] = st[h].T


def _hgrn_prompt(proj3, lb, gn):
    bsz, t, _ = proj3.shape
    tt = min(256, t)
    dc = HG_HEADS * HG_K
    kern = functools.partial(_hgrn_prompt_kernel, tt=tt)
    blk = lambda c: pl.BlockSpec((1, tt, dc), lambda i, j, c=c: (i, j, c))
    return pl.pallas_call(
        kern,
        grid=(bsz, t // tt),
        in_specs=[blk(0), blk(1), blk(2), blk(3),
                  pl.BlockSpec((1, dc), lambda i, j: (0, 0)),
                  pl.BlockSpec((1, HG_V), lambda i, j: (0, 0))],
        out_specs=[
            pl.BlockSpec((1, tt, dc), lambda i, j: (i, j, 0)),
            pl.BlockSpec((1, HG_HEADS, HG_K, HG_V), lambda i, j: (i, 0, 0, 0)),
        ],
        out_shape=[
            jax.ShapeDtypeStruct((bsz, t, dc), BF16),
            jax.ShapeDtypeStruct((bsz, HG_HEADS, HG_K, HG_V), F32),
        ],
        scratch_shapes=[pltpu.VMEM((HG_HEADS, HG_V, HG_K), F32)],
        compiler_params=_cparams("parallel", "arbitrary"),
        name="hgrn_prompt",
    )(proj3, proj3, proj3, proj3, lb, gn)


def _hgrn_sample_kernel(q_ref, f_ref, v_ref, g_ref, s_ref, lb_ref, gn_ref, o_ref, s_out, *, nbk):
    for h in range(HG_HEADS):
        hs = slice(h * HG_K, (h + 1) * HG_K)
        q, logg, k = _hg_gates(q_ref[:, hs], f_ref[:, hs], lb_ref[:, hs])
        dec = jnp.exp(logg)
        v = v_ref[:, hs]
        rows = []
        for n in range(nbk):
            col = lambda x: jnp.broadcast_to(x[n:n + 1, :], (HG_K, HG_K)).T
            s_new = col(dec) * s_ref[n, h] + col(k) * v[n:n + 1, :]
            s_out[n, h] = s_new
            rows.append(jnp.sum(col(q) * s_new, axis=0, keepdims=True))
        o = jnp.concatenate(rows, axis=0)
        o_ref[:, hs] = _hg_out(o, g_ref[:, hs], gn_ref[...]).astype(o_ref.dtype)


def _hgrn_sample(proj, s0, lb, gn):
    n = proj.shape[0]
    nbk = 8
    dc = HG_HEADS * HG_K
    kern = functools.partial(_hgrn_sample_kernel, nbk=nbk)
    blk = lambda c: pl.BlockSpec((nbk, dc), lambda i, c=c: (i, c))
    sblk = pl.BlockSpec((nbk, HG_HEADS, HG_K, HG_V), lambda i: (i, 0, 0, 0))
    return pl.pallas_call(
        kern,
        grid=(n // nbk,),
        in_specs=[blk(0), blk(1), blk(2), blk(3), sblk,
                  pl.BlockSpec((1, dc), lambda i: (0, 0)),
                  pl.BlockSpec((1, HG_V), lambda i: (0, 0))],
        out_specs=[pl.BlockSpec((nbk, dc), lambda i: (i, 0)), sblk],
        out_shape=[
            jax.ShapeDtypeStruct((n, dc), BF16),
            jax.ShapeDtypeStruct(s0.shape, F32),
        ],
        compiler_params=_cparams("parallel"),
        name="hgrn_sample",
    )(proj, proj, proj, proj, s0, lb, gn)


def _mix_ffn_kernel(h_ref, ya_ref, yb_ref, woa_ref, wob_ref, g_ref, wg_ref, wu_ref, wd_ref, o_ref,
                    h1_ref, xn_ref, acc_ref):
    k = pl.program_id(1)

    @pl.when(k == 0)
    def _():
        h1 = (h_ref[...] + jnp.dot(ya_ref[...], woa_ref[...], preferred_element_type=F32)
              + jnp.dot(yb_ref[...], wob_ref[...], preferred_element_type=F32))
        h1_ref[...] = h1
        xn_ref[...] = _rms(h1, g_ref[...]).astype(BF16)
        acc_ref[...] = jnp.zeros(acc_ref.shape, F32)

    xn = xn_ref[...]
    a = jnp.dot(xn, wg_ref[...], preferred_element_type=F32)
    b = jnp.dot(xn, wu_ref[...], preferred_element_type=F32)
    acc_ref[...] += jnp.dot((_silu(a) * b).astype(BF16), wd_ref[...], preferred_element_type=F32)

    @pl.when(k == pl.num_programs(1) - 1)
    def _():
        o_ref[...] = h1_ref[...] + acc_ref[...]


def _mix_ffn(h, ya, yb, woa, wob, g, wg, wu, wd):
    n, d = h.shape
    tm = min(512, n)
    tf = D_FF_PAD // 2
    row = lambda w: pl.BlockSpec((tm, w), lambda i, k: (i, 0))
    return pl.pallas_call(
        _mix_ffn_kernel,
        grid=(n // tm, D_FF_PAD // tf),
        in_specs=[
            row(d), row(D_A), row(D_B),
            pl.BlockSpec((D_A, d), lambda i, k: (0, 0)),
            pl.BlockSpec((D_B, d), lambda i, k: (0, 0)),
            pl.BlockSpec((1, d), lambda i, k: (0, 0)),
            pl.BlockSpec((d, tf), lambda i, k: (0, k)),
            pl.BlockSpec((d, tf), lambda i, k: (0, k)),
            pl.BlockSpec((tf, d), lambda i, k: (k, 0)),
        ],
        out_specs=row(d),
        out_shape=jax.ShapeDtypeStruct((n, d), F32),
        scratch_shapes=[pltpu.VMEM((tm, d), F32), pltpu.VMEM((tm, d), BF16), pltpu.VMEM((tm, d), F32)],
        compiler_params=_cparams("parallel", "arbitrary"),
        name="mix_ffn",
    )(h, ya, yb, woa, wob, g, wg, wu, wd)


def _mix_router_kernel(h_ref, y_ref, wo_ref, g_ref, r_ref, h1_ref, xn_ref, ids_ref, gts_ref):
    h1 = h_ref[...] + jnp.dot(y_ref[...], wo_ref[...], preferred_element_type=F32)
    h1_ref[...] = h1
    xn = _rms(h1, g_ref[...])
    xn_ref[...] = xn.astype(BF16)
    logits = jnp.dot(xn, r_ref[...], preferred_element_type=F32, precision=lax.Precision.HIGHEST)
    lane = lax.broadcasted_iota(I32, logits.shape, 1)
    neg = jnp.float32(-jnp.inf)
    logits = jnp.where(lane < N_EXP, logits, neg)
    m1 = jnp.max(logits, axis=-1, keepdims=True)
    i1 = jnp.min(jnp.where(logits == m1, lane, LANES), axis=-1, keepdims=True)
    rest = jnp.where(lane == i1, neg, logits)
    m2 = jnp.max(rest, axis=-1, keepdims=True)
    i2 = jnp.min(jnp.where(rest == m2, lane, LANES), axis=-1, keepdims=True)
    e2 = jnp.exp(m2 - m1)
    g1 = 1.0 / (1.0 + e2)
    g2 = e2 / (1.0 + e2)
    ids_ref[...] = jnp.where(lane == 0, i1, jnp.where(lane == 1, i2, 0))
    gts_ref[...] = jnp.where(lane == 0, g1, jnp.where(lane == 1, g2, 0.0))


def _mix_router(h, y, wo, g, router_pad):
    n, d = h.shape
    tm = min(512, n)
    row = lambda w: pl.BlockSpec((tm, w), lambda i: (i, 0))
    return pl.pallas_call(
        _mix_router_kernel,
        grid=(n // tm,),
        in_specs=[row(d), row(d),
                  pl.BlockSpec((d, d), lambda i: (0, 0)),
                  pl.BlockSpec((1, d), lambda i: (0, 0)),
                  pl.BlockSpec((d, LANES), lambda i: (0, 0))],
        out_specs=[row(d), row(d), row(LANES), row(LANES)],
        out_shape=[
            jax.ShapeDtypeStruct((n, d), F32),
            jax.ShapeDtypeStruct((n, d), BF16),
            jax.ShapeDtypeStruct((n, LANES), I32),
            jax.ShapeDtypeStruct((n, LANES), F32),
        ],
        compiler_params=_cparams("parallel"),
        name="mix_router",
    )(h, y, wo, g, router_pad)


def _dispatch_kernel(clo_ref, ncs_ref, x_hbm, pos_ref, gts_ref, xs_ref, rg_ref, xbuf, sem, *, tr, tc):
    i = pl.program_id(0)
    c0 = clo_ref[i]
    nc = ncs_ref[i]
    rows = i * tr + lax.broadcasted_iota(I32, (tr, 1), 0)

    def copy(c, slot):
        return pltpu.make_async_copy(x_hbm.at[pl.ds((c0 + c) * tc, tc), :], xbuf.at[slot], sem.at[slot])

    @pl.when(nc > 0)
    def _():
        copy(0, 0).start()

    def body(c, carry):
        acc, rg = carry
        slot = c % 2
        copy(c, slot).wait()

        @pl.when(c + 1 < nc)
        def _():
            copy(c + 1, 1 - slot).start()

        pc = pos_ref[c0 + c]
        gc = gts_ref[c0 + c]
        m0 = pc[0:1, :] == rows
        m1 = pc[1:2, :] == rows
        onehot = jnp.where(m0 | m1, 1.0, 0.0).astype(BF16)
        acc = acc + jnp.dot(onehot, xbuf[slot], preferred_element_type=F32)
        gsel = jnp.where(m0, gc[0:1, :], 0.0) + jnp.where(m1, gc[1:2, :], 0.0)
        rg = rg + jnp.sum(gsel, axis=-1, keepdims=True)
        return acc, rg

    acc, rg = lax.fori_loop(0, nc, body, (jnp.zeros((tr, D_MODEL), F32), jnp.zeros((tr, 1), F32)))
    xs_ref[...] = acc.astype(BF16)
    rg_ref[...] = rg


def _dispatch(x_all, pos_c, gts_c, c_lo, n_chunks, n_rows):
    tr = DISPATCH_TR
    nch, _, tc = pos_c.shape
    kern = functools.partial(_dispatch_kernel, tr=tr, tc=tc)
    grid_spec = pltpu.PrefetchScalarGridSpec(
        num_scalar_prefetch=2,
        grid=(n_rows // tr,),
        in_specs=[
            pl.BlockSpec(memory_space=pl.ANY),
            pl.BlockSpec((nch, 2, tc), lambda i, a, b: (0, 0, 0)),
            pl.BlockSpec((nch, 2, tc), lambda i, a, b: (0, 0, 0)),
        ],
        out_specs=[
            pl.BlockSpec((tr, D_MODEL), lambda i, a, b: (i, 0)),
            pl.BlockSpec((tr, 1), lambda i, a, b: (i, 0)),
        ],
        scratch_shapes=[pltpu.VMEM((2, tc, D_MODEL), BF16), pltpu.SemaphoreType.DMA((2,))],
    )
    return pl.pallas_call(
        kern,
        grid_spec=grid_spec,
        out_shape=[
            jax.ShapeDtypeStruct((n_rows, D_MODEL), BF16),
            jax.ShapeDtypeStruct((n_rows, 1), F32),
        ],
        compiler_params=_cparams("arbitrary"),
        name="moe_dispatch",
    )(c_lo, n_chunks, x_all, pos_c, gts_c)


def _expert_kernel(te_ref, nu_ref, xs_ref, rg_ref, wg_ref, wu_ref, wd_ref, ys_ref, acc_ref):
    i = pl.program_id(0)
    k = pl.program_id(1)
    used = i < nu_ref[0]

    @pl.when(k == 0)
    def _():
        acc_ref[...] = jnp.zeros(acc_ref.shape, F32)

    @pl.when(used)
    def _():
        x = xs_ref[...]
        a = jnp.dot(x, wg_ref[0], preferred_element_type=F32)
        b = jnp.dot(x, wu_ref[0], preferred_element_type=F32)
        acc_ref[...] += jnp.dot((_silu(a) * b).astype(BF16), wd_ref[0], preferred_element_type=F32)

    @pl.when(k == pl.num_programs(1) - 1)
    def _():
        ys_ref[...] = (acc_ref[...] * rg_ref[...]).astype(ys_ref.dtype)


def _experts(xs, rg, tile_expert, n_used, wg, wu, wd):
    n_rows = xs.shape[0]
    tm, tk = MOE_TM, MOE_TK
    nk = D_EXP // tk

    def kk(i, k, nu):
        return jnp.where(i < nu[0], k, nk - 1)

    grid_spec = pltpu.PrefetchScalarGridSpec(
        num_scalar_prefetch=2,
        grid=(n_rows // tm, nk),
        in_specs=[
            pl.BlockSpec((tm, D_MODEL), lambda i, k, te, nu: (i, 0)),
            pl.BlockSpec((tm, 1), lambda i, k, te, nu: (i, 0)),
            pl.BlockSpec((1, D_MODEL, tk), lambda i, k, te, nu: (te[i], 0, kk(i, k, nu))),
            pl.BlockSpec((1, D_MODEL, tk), lambda i, k, te, nu: (te[i], 0, kk(i, k, nu))),
            pl.BlockSpec((1, tk, D_MODEL), lambda i, k, te, nu: (te[i], kk(i, k, nu), 0)),
        ],
        out_specs=pl.BlockSpec((tm, D_MODEL), lambda i, k, te, nu: (i, 0)),
        scratch_shapes=[pltpu.VMEM((tm, D_MODEL), F32)],
    )
    return pl.pallas_call(
        _expert_kernel,
        grid_spec=grid_spec,
        out_shape=jax.ShapeDtypeStruct((n_rows, D_MODEL), BF16),
        compiler_params=_cparams("arbitrary", "arbitrary"),
        name="moe_experts",
    )(tile_expert, n_used, xs, rg, wg, wu, wd)


def _combine_kernel(win_ref, nsub_ref, h_ref, pos_ref, ids_ref, ys_hbm, o_ref, buf, sem, *, tt):
    i = pl.program_id(0)
    w = COMBINE_W

    def copy(e, j, slot):
        start = pl.multiple_of(win_ref[i * N_EXP + e] + j * w, 16)
        return pltpu.make_async_copy(ys_hbm.at[pl.ds(start, w), :], buf.at[slot], sem.at[slot])

    def slot0(e):
        s = 0
        for e2 in range(e):
            s = s + nsub_ref[i * N_EXP + e2]
        return s

    for e in range(N_EXP):
        s0 = slot0(e)

        def issue(j, c, e=e, s0=s0):
            copy(e, j, s0 + j).start()
            return c

        lax.fori_loop(0, nsub_ref[i * N_EXP + e], issue, 0)

    lane = lax.broadcasted_iota(I32, (1, w), 1)
    p0, p1 = pos_ref[:, 0:1], pos_ref[:, 1:2]
    e0, e1 = ids_ref[:, 0:1], ids_ref[:, 1:2]
    acc = h_ref[...]
    for e in range(N_EXP):
        s0 = slot0(e)

        def consume(j, acc, e=e, s0=s0):
            copy(e, j, s0 + j).wait()
            rows = win_ref[i * N_EXP + e] + j * w + lane
            hit = ((e0 == e) & (p0 == rows)) | ((e1 == e) & (p1 == rows))
            onehot = jnp.where(hit, 1.0, 0.0).astype(BF16)
            return acc + jnp.dot(onehot, buf[s0 + j], preferred_element_type=F32)

        acc = lax.fori_loop(0, nsub_ref[i * N_EXP + e], consume, acc)
    o_ref[...] = acc


def _combine(h1, pos, ids, ys, win, nsub, tt):
    n = h1.shape[0]
    kern = functools.partial(_combine_kernel, tt=tt)
    grid_spec = pltpu.PrefetchScalarGridSpec(
        num_scalar_prefetch=2,
        grid=(n // tt,),
        in_specs=[
            pl.BlockSpec((tt, D_MODEL), lambda i, a, b: (i, 0)),
            pl.BlockSpec((tt, 2), lambda i, a, b: (i, 0)),
            pl.BlockSpec((tt, 2), lambda i, a, b: (i, 0)),
            pl.BlockSpec(memory_space=pl.ANY),
        ],
        out_specs=pl.BlockSpec((tt, D_MODEL), lambda i, a, b: (i, 0)),
        scratch_shapes=[pltpu.VMEM((COMBINE_SLOTS, COMBINE_W, D_MODEL), BF16),
                        pltpu.SemaphoreType.DMA((COMBINE_SLOTS,))],
    )
    return pl.pallas_call(
        kern,
        grid_spec=grid_spec,
        out_shape=jax.ShapeDtypeStruct((n, D_MODEL), F32),
        compiler_params=_cparams("arbitrary"),
        name="moe_combine",
    )(win, nsub, h1, pos, ids, ys)


def _combine_plan(first, count, tt):
    win = (first // 16) * 16
    nsub = jnp.where(count > 0, (first - win + count + COMBINE_W - 1) // COMBINE_W, 0)
    return win.reshape(-1).astype(I32), nsub.reshape(-1).astype(I32)


def _moe(h1_p, h1_s, xn_p, xn_s, ids_p, ids_s, gts_p, gts_s, wg, wu, wd):
    n_p, n_s = h1_p.shape[0], h1_s.shape[0]
    n_tok = n_p + n_s
    tm, tr = MOE_TM, DISPATCH_TR
    tc = DISPATCH_TC if n_tok % DISPATCH_TC == 0 else LANES
    n_tiles = (2 * n_tok + N_EXP * (tm - 1)) // tm + 1
    n_rows = n_tiles * tm

    x_all = jnp.concatenate([xn_p, xn_s], axis=0)
    e = jnp.concatenate([ids_p[:, :2], ids_s[:, :2]], axis=0)
    gt = jnp.concatenate([gts_p[:, :2], gts_s[:, :2]], axis=0)
    hit = jnp.sum((e[:, :, None] == jnp.arange(N_EXP, dtype=I32)).astype(I32), axis=1)
    cum = jnp.cumsum(hit, axis=0)
    rank = cum - hit
    counts = cum[-1]
    padded = ((counts + tm - 1) // tm) * tm
    ends = jnp.cumsum(padded)
    offs = ends - padded
    pos = offs[e] + jnp.take_along_axis(rank, e, axis=1)

    tile_expert = jnp.minimum(jnp.searchsorted(ends, jnp.arange(n_tiles, dtype=I32) * tm, side="right"),
                              N_EXP - 1).astype(I32)
    n_used = (ends[-1] // tm).astype(I32).reshape(1)

    r0 = jnp.arange(n_rows // tr, dtype=I32) * tr
    re_ = jnp.minimum(jnp.searchsorted(ends, r0, side="right"), N_EXP - 1)
    rho_lo = r0 - offs[re_]
    cnt_e = counts[re_]
    valid = (r0 < ends[-1]) & (rho_lo < cnt_e)
    rho_hi = jnp.minimum(rho_lo + tr, cnt_e) - 1
    cum_e = cum.T[re_]
    tok_lo = jnp.sum((cum_e <= rho_lo[:, None]).astype(I32), axis=1)
    tok_hi = jnp.sum((cum_e <= rho_hi[:, None]).astype(I32), axis=1)
    c_lo = jnp.where(valid, tok_lo // tc, 0).astype(I32)
    n_chunks = jnp.where(valid, tok_hi // tc - tok_lo // tc + 1, 0).astype(I32)

    chunked = lambda a: jnp.transpose(a.T.reshape(2, n_tok // tc, tc), (1, 0, 2))
    xs, rg = _dispatch(x_all, chunked(pos), chunked(gt), c_lo, n_chunks, n_rows)
    ys = _experts(xs, rg, tile_expert, n_used, wg, wu, wd)

    def plan(lo, n, tt):
        first = offs[None, :] + rank[lo:lo + n:tt]
        last = cum[lo + tt - 1:lo + n:tt]
        return _combine_plan(first, last - rank[lo:lo + n:tt], tt)

    tt_p, tt_s = min(256, n_p), min(128, n_s)
    out_p = _combine(h1_p, pos[:n_p], e[:n_p], ys, *plan(0, n_p, tt_p), tt_p)
    out_s = _combine(h1_s, pos[n_p:], e[n_p:], ys, *plan(n_p, n_s, tt_s), tt_s)
    return out_p, out_s


def _final_norm_kernel(x_ref, g_ref, o_ref):
    o_ref[...] = _rms(x_ref[...], g_ref[...])


def _final_norm(x, g):
    n, d = x.shape
    tm = min(1024, n)
    return pl.pallas_call(
        _final_norm_kernel,
        grid=(n // tm,),
        in_specs=[pl.BlockSpec((tm, d), lambda i: (i, 0)), pl.BlockSpec((1, d), lambda i: (0, 0))],
        out_specs=pl.BlockSpec((tm, d), lambda i: (i, 0)),
        out_shape=jax.ShapeDtypeStruct((n, d), F32),
        compiler_params=_cparams("parallel"),
        name="final_norm",
    )(x, g)


def kernel(x_prompt, x_sample, state_conv, state_ssm_re, state_ssm_im, state_hgrn, norm_mix, norm_ffn, norm_final, ab_w_in, ab_w_out, conv_w, conv_b, conv_ln_g, conv_ln_b, ssm_a_re, ssm_a_im, ssm_log_dt, ssm_b_re, ssm_b_im, ssm_c_re, ssm_c_im, ssm_d, ssm_glu_w, ssm_glu_b, hg_w_in, hg_lb_logits, hg_gnorm, hg_w_out, ffn_w_gate, ffn_w_up, ffn_w_down, moe_router, moe_w_gate, moe_w_up, moe_w_down):
    bsz, seq, d = x_prompt.shape
    n_s = x_sample.shape[0]
    depth = norm_mix.shape[0]
    hp = x_prompt.reshape(bsz * seq, d)
    hs = x_sample.reshape(n_s, d)

    lbs = jax.nn.softmax(hg_lb_logits.astype(F32), axis=0)
    lbs = jnp.cumsum(lbs, axis=0) - lbs[0:1]

    conv_p, re_p, im_p, hg_p = [], [], [], []
    conv_s, re_s, im_s, hg_s = [], [], [], []
    zero_state = jnp.zeros((bsz, S5_STATE), F32)
    row = lambda v: v.reshape(1, -1).astype(F32)

    for l in range(depth):
        j = l // 2
        g_mix, g_ffn = row(norm_mix[l]), row(norm_ffn[l])
        if l % 2 == 0:
            w_in = ab_w_in[j].astype(BF16)
            woa = ab_w_out[j, :D_A].astype(BF16)
            wob = ab_w_out[j, D_A:].astype(BF16)
            cw, cb, clg, clb = conv_w[j].astype(F32), row(conv_b[j]), row(conv_ln_g[j]), row(conv_ln_b[j])
            abar_re, abar_im, bcat, ccat = _s5_params(
                ssm_a_re[j].astype(F32), ssm_a_im[j].astype(F32), ssm_log_dt[j].astype(F32),
                ssm_b_re[j].astype(F32), ssm_b_im[j].astype(F32), ssm_c_re[j].astype(F32), ssm_c_im[j].astype(F32))
            sd, sgw, sgb = row(ssm_d[j]), ssm_glu_w[j].astype(BF16), row(ssm_glu_b[j])
            pad = ((0, 0), (0, D_FF_PAD - D_FF))
            wg = jnp.pad(ffn_w_gate[j], pad).astype(BF16)
            wu = jnp.pad(ffn_w_up[j], pad).astype(BF16)
            wd = jnp.pad(ffn_w_down[j], (pad[1], pad[0])).astype(BF16)

            proj = _norm_matmul(hp, g_mix, w_in, tn=w_in.shape[1])
            proj3 = proj.reshape(bsz, seq, -1)
            ya, cb_new = _conv_prompt(proj3, cw, cb, clg, clb)
            u_t = jnp.transpose(proj3[:, :, 2 * D_A:], (1, 0, 2))
            yb_t, hr, hi = _s5(u_t, zero_state, zero_state, abar_re, abar_im, bcat, ccat, sd, sgw, sgb,
                               tt=min(128, seq))
            yb = jnp.transpose(yb_t, (1, 0, 2)).reshape(bsz * seq, D_B)
            hp = _mix_ffn(hp, ya.reshape(bsz * seq, D_A), yb, woa, wob, g_ffn, wg, wu, wd)
            conv_p.append(cb_new)
            re_p.append(hr.reshape(bsz, S5_G, S5_P))
            im_p.append(hi.reshape(bsz, S5_G, S5_P))

            proj = _norm_matmul(hs, g_mix, w_in, tn=w_in.shape[1])
            ya, nb_t = _conv_sample(proj, jnp.transpose(state_conv[j].astype(F32), (1, 0, 2)), cw, cb, clg, clb)
            yb_t, hr, hi = _s5(proj[:, 2 * D_A:].reshape(1, n_s, D_B),
                               state_ssm_re[j].astype(F32).reshape(n_s, S5_STATE),
                               state_ssm_im[j].astype(F32).reshape(n_s, S5_STATE),
                               abar_re, abar_im, bcat, ccat, sd, sgw, sgb, tt=1)
            hs = _mix_ffn(hs, ya, yb_t.reshape(n_s, D_B), woa, wob, g_ffn, wg, wu, wd)
            conv_s.append(jnp.transpose(nb_t, (1, 0, 2)))
            re_s.append(hr.reshape(n_s, S5_G, S5_P))
            im_s.append(hi.reshape(n_s, S5_G, S5_P))
        else:
            w_in = hg_w_in[j].astype(BF16)
            wo = hg_w_out[j].astype(BF16)
            lb, gn = row(lbs[j]), row(hg_gnorm[j])
            router_pad = jnp.pad(moe_router[j].astype(F32), ((0, 0), (0, LANES - N_EXP)))
            wg, wu, wd = moe_w_gate[j].astype(BF16), moe_w_up[j].astype(BF16), moe_w_down[j].astype(BF16)

            proj = _norm_matmul(hp, g_mix, w_in, tn=1024)
            o, s_new = _hgrn_prompt(proj.reshape(bsz, seq, -1), lb, gn)
            h1_p, xn_p, ids_p, gts_p = _mix_router(hp, o.reshape(bsz * seq, d), wo, g_ffn, router_pad)
            hg_p.append(s_new)

            proj = _norm_matmul(hs, g_mix, w_in, tn=1024)
            o, s_new = _hgrn_sample(proj, state_hgrn[j].astype(F32), lb, gn)
            h1_s, xn_s, ids_s, gts_s = _mix_router(hs, o, wo, g_ffn, router_pad)
            hg_s.append(s_new)

            hp, hs = _moe(h1_p, h1_s, xn_p, xn_s, ids_p, ids_s, gts_p, gts_s, wg, wu, wd)

    g_fin = row(norm_final)
    y_prompt = _final_norm(hp, g_fin).reshape(bsz, seq, d)
    y_sample = _final_norm(hs, g_fin).reshape(n_s, 1, d)
    return (y_prompt, y_sample, jnp.stack(conv_p), jnp.stack(re_p), jnp.stack(im_p), jnp.stack(hg_p),
            jnp.stack(conv_s), jnp.stack(re_s), jnp.stack(im_s), jnp.stack(hg_s))
```

```python
import functools
import math

import jax
import jax.numpy as jnp
from jax import lax
from jax.experimental import pallas as pl
from jax.experimental.pallas import tpu as pltpu

F32 = jnp.float32
BF16 = jnp.bfloat16
I32 = jnp.int32

EPS = 1e-6
LN_EPS = 1e-5
FORGET_FLOOR = 1e-30

D_MODEL = 1024
D_A = 512
D_B = 512
CONV_W = 31
CONV_HALO = 32
S5_G = 32
S5_P = 64
S5_GH = 16
S5_STATE = S5_G * S5_P
S5_CHUNKS = 4
HG_HEADS = 8
HG_K = 128
HG_V = 128
HG_CHUNK = 64
D_FF = 2752
D_FF_PAD = 2816
N_EXP = 8
D_EXP = 3584
LANES = 128
VMEM_LIMIT_BYTES = 56 * 1024 * 1024
EXP_CLAMP = 80.0

MOE_TM = 512


def _cparams(*sem):
    return pltpu.CompilerParams(dimension_semantics=sem, vmem_limit_bytes=VMEM_LIMIT_BYTES)


def _sigmoid(x):
    return 1.0 / (1.0 + jnp.exp(-x))


def _silu(x):
    return x * _sigmoid(x)


def _rms(x, g):
    return x * lax.rsqrt(jnp.mean(x * x, axis=-1, keepdims=True) + EPS) * g


def _bdot(a, b):
    return jnp.dot(a.astype(BF16), b.astype(BF16), preferred_element_type=F32)


def _norm_matmul_kernel(x_ref, g_ref, w_ref, o_ref, xn_ref):
    @pl.when(pl.program_id(1) == 0)
    def _():
        xn_ref[...] = _rms(x_ref[...], g_ref[...]).astype(BF16)

    o_ref[...] = jnp.dot(xn_ref[...], w_ref[...], preferred_element_type=F32).astype(o_ref.dtype)


def _norm_matmul(x, g, w, tn, out_dtype=F32):
    n, d = x.shape
    nout = w.shape[1]
    tm = min(1024, n)
    return pl.pallas_call(
        _norm_matmul_kernel,
        grid=(n // tm, nout // tn),
        in_specs=[
            pl.BlockSpec((tm, d), lambda i, j: (i, 0)),
            pl.BlockSpec((1, d), lambda i, j: (0, 0)),
            pl.BlockSpec((d, tn), lambda i, j: (0, j)),
        ],
        out_specs=pl.BlockSpec((tm, tn), lambda i, j: (i, j)),
        out_shape=jax.ShapeDtypeStruct((n, nout), out_dtype),
        scratch_shapes=[pltpu.VMEM((tm, d), BF16)],
        compiler_params=_cparams("parallel", "arbitrary"),
        name="norm_matmul",
    )(x, g, w)


def _conv_post(y, b, lg, lb):
    y = y + b
    mu = jnp.mean(y, axis=-1, keepdims=True)
    yc = y - mu
    var = jnp.mean(yc * yc, axis=-1, keepdims=True)
    return _silu(yc * lax.rsqrt(var + LN_EPS) * lg + lb)


def _conv_prompt_kernel(av_ref, ag_ref, w_ref, b_ref, lg_ref, lb_ref, y_ref, nb_ref, ubuf, *, tt, rb):
    t = pl.program_id(1)

    @pl.when(t == 0)
    def _():
        ubuf[0:CONV_HALO, :] = jnp.zeros((CONV_HALO, D_A), F32)

    @pl.when(t > 0)
    def _():
        ubuf[0:CONV_HALO, :] = ubuf[tt:tt + CONV_HALO, :]

    ubuf[CONV_HALO:CONV_HALO + tt, :] = av_ref[0].astype(F32) * _sigmoid(ag_ref[0].astype(F32))
    off = CONV_HALO - (CONV_W - 1)
    span = rb + CONV_HALO
    for r in range(tt // rb):
        win = ubuf[r * rb:r * rb + span, :]
        acc = jnp.zeros((rb, D_A), F32)
        for p in range(8):
            wp = win if p == 0 else pltpu.roll(win, span - p, 0)
            for q in range(CONV_HALO // 8 + 1):
                k = 8 * q + p - off
                if 0 <= k < CONV_W:
                    acc = acc + w_ref[k:k + 1, :] * wp[8 * q:8 * q + rb, :]
        y_ref[0, r * rb:(r + 1) * rb, :] = _conv_post(acc, b_ref[...], lg_ref[...], lb_ref[...]).astype(y_ref.dtype)

    @pl.when(t == pl.num_programs(1) - 1)
    def _():
        nb_ref[0] = ubuf[tt + off:tt + CONV_HALO, :]


def _conv_prompt(proj3, w, b, lg, lb):
    bsz, t, _ = proj3.shape
    tt = min(256, t)
    kern = functools.partial(_conv_prompt_kernel, tt=tt, rb=32)
    vec = pl.BlockSpec((1, D_A), lambda i, j: (0, 0))
    return pl.pallas_call(
        kern,
        grid=(bsz, t // tt),
        in_specs=[
            pl.BlockSpec((1, tt, D_A), lambda i, j: (i, j, 0)),
            pl.BlockSpec((1, tt, D_A), lambda i, j: (i, j, 1)),
            pl.BlockSpec((CONV_W, D_A), lambda i, j: (0, 0)),
            vec, vec, vec,
        ],
        out_specs=[
            pl.BlockSpec((1, tt, D_A), lambda i, j: (i, j, 0)),
            pl.BlockSpec((1, CONV_W - 1, D_A), lambda i, j: (i, 0, 0)),
        ],
        out_shape=[
            jax.ShapeDtypeStruct((bsz, t, D_A), BF16),
            jax.ShapeDtypeStruct((bsz, CONV_W - 1, D_A), F32),
        ],
        scratch_shapes=[pltpu.VMEM((CONV_HALO + tt, D_A), F32)],
        compiler_params=_cparams("parallel", "arbitrary"),
        name="conv_prompt",
    )(proj3, proj3, w, b, lg, lb)


def _conv_sample_kernel(av_ref, ag_ref, buf_ref, w_ref, b_ref, lg_ref, lb_ref, y_ref, nb_ref):
    u = av_ref[...] * _sigmoid(ag_ref[...])
    acc = w_ref[CONV_W - 1:CONV_W, :] * u
    for k in range(CONV_W - 1):
        acc = acc + w_ref[k:k + 1, :] * buf_ref[k]
    y_ref[...] = _conv_post(acc, b_ref[...], lg_ref[...], lb_ref[...]).astype(y_ref.dtype)
    for k in range(CONV_W - 2):
        nb_ref[k] = buf_ref[k + 1]
    nb_ref[CONV_W - 2] = u


def _conv_sample(proj, buf_t, w, b, lg, lb):
    n = proj.shape[0]
    nbk = min(32, n)
    vec = pl.BlockSpec((1, D_A), lambda i: (0, 0))
    return pl.pallas_call(
        _conv_sample_kernel,
        grid=(n // nbk,),
        in_specs=[
            pl.BlockSpec((nbk, D_A), lambda i: (i, 0)),
            pl.BlockSpec((nbk, D_A), lambda i: (i, 1)),
            pl.BlockSpec((CONV_W - 1, nbk, D_A), lambda i: (0, i, 0)),
            pl.BlockSpec((CONV_W, D_A), lambda i: (0, 0)),
            vec, vec, vec,
        ],
        out_specs=[
            pl.BlockSpec((nbk, D_A), lambda i: (i, 0)),
            pl.BlockSpec((CONV_W - 1, nbk, D_A), lambda i: (0, i, 0)),
        ],
        out_shape=[
            jax.ShapeDtypeStruct((n, D_A), BF16),
            jax.ShapeDtypeStruct((CONV_W - 1, n, D_A), F32),
        ],
        compiler_params=_cparams("parallel"),
        name="conv_sample",
    )(proj, proj, buf_t, w, b, lg, lb)


def _s5_kernel(u_ref, h0r_ref, h0i_ref, ar_ref, ai_ref, bcat_ref, ccat_ref, d_ref, gw_ref, gb_ref,
               y_ref, hr_out, hi_out, sre, sim, cre, cim, rt, *, tt, nb, lw, seq_major):
    i = pl.program_id(0)
    m = tt * nb
    cw = S5_STATE // S5_CHUNKS

    @pl.when(i == 0)
    def _():
        cre[...] = h0r_ref[...]
        cim[...] = h0i_ref[...]

    if seq_major:
        for b in range(nb):
            rt[:, b, :] = u_ref[b].astype(F32)
        u = rt[...].reshape(m, D_B)
    else:
        u = u_ref[...].reshape(m, D_B)
    ub = u.astype(BF16)
    for c in range(S5_CHUNKS):
        bu = jnp.dot(ub[:, c * LANES:(c + 1) * LANES], bcat_ref[c], preferred_element_type=F32)
        sre[:, :, c * cw:(c + 1) * cw] = bu[:, :cw].reshape(tt, nb, cw)
        sim[:, :, c * cw:(c + 1) * cw] = bu[:, cw:].reshape(tt, nb, cw)

    for c in range(S5_STATE // lw):
        ls = slice(c * lw, (c + 1) * lw)
        ar = jnp.broadcast_to(ar_ref[:, ls], (nb, lw))
        ai = jnp.broadcast_to(ai_ref[:, ls], (nb, lw))

        def body(t, carry, ls=ls, ar=ar, ai=ai):
            hr, hi = carry
            nr = ar * hr - ai * hi + sre[t, :, ls]
            ni = ar * hi + ai * hr + sim[t, :, ls]
            sre[t, :, ls] = nr
            sim[t, :, ls] = ni
            return nr, ni

        hr, hi = lax.fori_loop(0, tt, body, (cre[:, ls], cim[:, ls]), unroll=min(tt, 8))
        cre[:, ls] = hr
        cim[:, ls] = hi

    hre = sre[...].reshape(m, S5_STATE)
    him = sim[...].reshape(m, S5_STATE)
    ys = []
    for c in range(S5_CHUNKS):
        hcat = jnp.concatenate([hre[:, c * cw:(c + 1) * cw], him[:, c * cw:(c + 1) * cw]], axis=1)
        ys.append(jnp.dot(hcat.astype(BF16), ccat_ref[c], preferred_element_type=F32))
    y = jnp.concatenate(ys, axis=1) + d_ref[...] * u
    y = 0.5 * y * (1.0 + lax.erf(y * (1.0 / math.sqrt(2.0))))
    z = jnp.dot(y.astype(BF16), gw_ref[...], preferred_element_type=F32) + gb_ref[...]
    out = (y * _sigmoid(z)).reshape(tt, nb, D_B)
    if seq_major:
        rt[...] = out
        for b in range(nb):
            y_ref[b] = rt[:, b, :].astype(y_ref.dtype)
    else:
        y_ref[...] = out.astype(y_ref.dtype)

    @pl.when(i == pl.num_programs(0) - 1)
    def _():
        hr_out[...] = cre[...]
        hi_out[...] = cim[...]


def _s5(u, h0r, h0i, abar_re, abar_im, bcat, ccat, d, glu_w, glu_b, tt, seq_major):
    if seq_major:
        nb, t, c = u.shape
        u_spec = pl.BlockSpec((nb, tt, D_B), lambda i: (0, i, c // D_B - 1))
        y_spec = pl.BlockSpec((nb, tt, D_B), lambda i: (0, i, 0))
        y_shape = (nb, t, D_B)
    else:
        t, nb, _ = u.shape
        u_spec = pl.BlockSpec((tt, nb, D_B), lambda i: (i, 0, 0))
        y_spec = u_spec
        y_shape = (t, nb, D_B)
    lw = max(LANES, min(512, 8 * 1024 // nb))
    kern = functools.partial(_s5_kernel, tt=tt, nb=nb, lw=lw, seq_major=seq_major)
    full = lambda shape: pl.BlockSpec(shape, lambda i: (0,) * len(shape))
    return pl.pallas_call(
        kern,
        grid=(t // tt,),
        in_specs=[
            u_spec,
            full((nb, S5_STATE)), full((nb, S5_STATE)),
            full((1, S5_STATE)), full((1, S5_STATE)),
            full(bcat.shape), full(ccat.shape),
            full((1, D_B)), full((D_B, D_B)), full((1, D_B)),
        ],
        out_specs=[
            y_spec,
            full((nb, S5_STATE)), full((nb, S5_STATE)),
        ],
        out_shape=[
            jax.ShapeDtypeStruct(y_shape, BF16),
            jax.ShapeDtypeStruct((nb, S5_STATE), F32),
            jax.ShapeDtypeStruct((nb, S5_STATE), F32),
        ],
        scratch_shapes=[
            pltpu.VMEM((tt, nb, S5_STATE), F32), pltpu.VMEM((tt, nb, S5_STATE), F32),
            pltpu.VMEM((nb, S5_STATE), F32), pltpu.VMEM((nb, S5_STATE), F32),
            pltpu.VMEM((tt, nb, D_B), F32),
        ],
        compiler_params=_cparams("arbitrary"),
        name="s5",
    )(u, h0r, h0i, abar_re, abar_im, bcat, ccat, d, glu_w, glu_b)


def _s5_params(a_re, a_im, log_dt, b_re, b_im, c_re, c_im):
    dt = jnp.exp(log_dt)[:, None]
    mag = jnp.exp(dt * a_re)
    ang = dt * a_im
    abar_re, abar_im = mag * jnp.cos(ang), mag * jnp.sin(ang)
    den = a_re * a_re + a_im * a_im
    nr, ni = abar_re - 1.0, abar_im
    coef_re = (nr * a_re + ni * a_im) / den
    coef_im = (ni * a_re - nr * a_im) / den
    bbar_re = coef_re[..., None] * b_re - coef_im[..., None] * b_im
    bbar_im = coef_re[..., None] * b_im + coef_im[..., None] * b_re
    gpc = S5_G // S5_CHUNKS
    eye = jnp.eye(gpc, dtype=F32)

    def bblk(x):
        x = x.reshape(S5_CHUNKS, gpc, S5_P, S5_GH)
        return jnp.einsum("cgph,gk->cghkp", x, eye).reshape(S5_CHUNKS, gpc * S5_GH, gpc * S5_P)

    def cblk(x):
        x = x.reshape(S5_CHUNKS, gpc, S5_GH, S5_P)
        return jnp.einsum("cghp,gk->cgpkh", x, eye).reshape(S5_CHUNKS, gpc * S5_P, gpc * S5_GH)

    bcat = jnp.concatenate([bblk(bbar_re), bblk(bbar_im)], axis=2).astype(BF16)
    ccat = jnp.concatenate([cblk(c_re), -cblk(c_im)], axis=1).astype(BF16)
    return abar_re.reshape(1, S5_STATE), abar_im.reshape(1, S5_STATE), bcat, ccat


def _hg_gates(q, f, lb):
    qf = _silu(q)
    forget = lb + (1.0 - lb) * _sigmoid(f)
    logg = jnp.log(jnp.maximum(forget, FORGET_FLOOR))
    kf = (1.0 - lb) * _sigmoid(-f)
    return qf, logg, kf


def _hg_out(o, g, gn):
    return o * lax.rsqrt(jnp.mean(o * o, axis=-1, keepdims=True) + EPS) * gn * _silu(g)


def _cumsum_rows(x, tri):
    hi = x.astype(BF16)
    r1 = x - hi.astype(F32)
    mid = r1.astype(BF16)
    lo = (r1 - mid.astype(F32)).astype(BF16)
    dot = lambda p: jnp.dot(tri, p, preferred_element_type=F32)
    return dot(hi) + dot(mid) + dot(lo)


def _hgrn_prompt_kernel(q_ref, f_ref, v_ref, g_ref, lb_ref, gn_ref, o_ref, s_out, st, *, tt):
    t = pl.program_id(1)
    L = HG_CHUNK

    @pl.when(t == 0)
    def _():
        st[...] = jnp.zeros(st.shape, F32)

    row = lax.broadcasted_iota(I32, (L, L), 0)
    col = lax.broadcasted_iota(I32, (L, L), 1)
    causal = row >= col
    tri = causal.astype(BF16)
    for h in range(HG_HEADS):
        hs = slice(h * HG_K, (h + 1) * HG_K)
        lb = lb_ref[:, hs]
        for c in range(tt // L):
            rs = slice(c * L, (c + 1) * L)
            q, logg, k = _hg_gates(q_ref[0, rs, hs].astype(F32), f_ref[0, rs, hs].astype(F32), lb)
            v = v_ref[0, rs, hs].astype(F32)
            gc = _cumsum_rows(logg, tri)
            gmid = gc[L // 2 - 1:L // 2, :]
            glast = gc[L - 1:L, :]
            qe = q * jnp.exp(jnp.minimum(gc - gmid, EXP_CLAMP))
            ke = k * jnp.exp(jnp.minimum(gmid - gc, EXP_CLAMP))
            sc = lax.dot_general(qe.astype(BF16), ke.astype(BF16), (((1,), (1,)), ((), ())),
                                 preferred_element_type=F32)
            sc = jnp.where(causal, sc, 0.0)
            s_t = st[h]
            o = _bdot(sc, v) + lax.dot_general((q * jnp.exp(gc)).astype(BF16), s_t.astype(BF16),
                                               (((1,), (1,)), ((), ())), preferred_element_type=F32)
            kd = k * jnp.exp(glast - gc)
            st[h] = jnp.exp(glast) * s_t + lax.dot_general(v.astype(BF16), kd.astype(BF16),
                                                           (((0,), (0,)), ((), ())),
                                                           preferred_element_type=F32)
            o_ref[0, rs, hs] = _hg_out(o, g_ref[0, rs, hs].astype(F32), gn_ref[...]).astype(o_ref.dtype)

    @pl.when(t == pl.num_programs(1) - 1)
    def _():
        for h in range(HG_HEADS):
            s_out[0, h] = st[h].T


def _hgrn_prompt(proj3, lb, gn):
    bsz, t, _ = proj3.shape
    tt = min(256, t)
    dc = HG_HEADS * HG_K
    kern = functools.partial(_hgrn_prompt_kernel, tt=tt)
    blk = lambda c: pl.BlockSpec((1, tt, dc), lambda i, j, c=c: (i, j, c))
    return pl.pallas_call(
        kern,
        grid=(bsz, t // tt),
        in_specs=[blk(0), blk(1), blk(2), blk(3),
                  pl.BlockSpec((1, dc), lambda i, j: (0, 0)),
                  pl.BlockSpec((1, HG_V), lambda i, j: (0, 0))],
        out_specs=[
            pl.BlockSpec((1, tt, dc), lambda i, j: (i, j, 0)),
            pl.BlockSpec((1, HG_HEADS, HG_K, HG_V), lambda i, j: (i, 0, 0, 0)),
        ],
        out_shape=[
            jax.ShapeDtypeStruct((bsz, t, dc), BF16),
            jax.ShapeDtypeStruct((bsz, HG_HEADS, HG_K, HG_V), F32),
        ],
        scratch_shapes=[pltpu.VMEM((HG_HEADS, HG_V, HG_K), F32)],
        compiler_params=_cparams("parallel", "arbitrary"),
        name="hgrn_prompt",
    )(proj3, proj3, proj3, proj3, lb, gn)


def _hgrn_sample_kernel(q_ref, f_ref, v_ref, g_ref, s_ref, lb_ref, gn_ref, o_ref, s_out, *, nbk):
    for h in range(HG_HEADS):
        hs = slice(h * HG_K, (h + 1) * HG_K)
        q, logg, k = _hg_gates(q_ref[:, hs], f_ref[:, hs], lb_ref[:, hs])
        dec = jnp.exp(logg)
        v = v_ref[:, hs]
        rows = []
        for n in range(nbk):
            col = lambda x: jnp.broadcast_to(x[n:n + 1, :], (HG_K, HG_K)).T
            s_new = col(dec) * s_ref[n, h] + col(k) * v[n:n + 1, :]
            s_out[n, h] = s_new
            rows.append(jnp.sum(col(q) * s_new, axis=0, keepdims=True))
        o = jnp.concatenate(rows, axis=0)
        o_ref[:, hs] = _hg_out(o, g_ref[:, hs], gn_ref[...]).astype(o_ref.dtype)


def _hgrn_sample(proj, s0, lb, gn):
    n = proj.shape[0]
    nbk = 8
    dc = HG_HEADS * HG_K
    kern = functools.partial(_hgrn_sample_kernel, nbk=nbk)
    blk = lambda c: pl.BlockSpec((nbk, dc), lambda i, c=c: (i, c))
    sblk = pl.BlockSpec((nbk, HG_HEADS, HG_K, HG_V), lambda i: (i, 0, 0, 0))
    return pl.pallas_call(
        kern,
        grid=(n // nbk,),
        in_specs=[blk(0), blk(1), blk(2), blk(3), sblk,
                  pl.BlockSpec((1, dc), lambda i: (0, 0)),
                  pl.BlockSpec((1, HG_V), lambda i: (0, 0))],
        out_specs=[pl.BlockSpec((nbk, dc), lambda i: (i, 0)), sblk],
        out_shape=[
            jax.ShapeDtypeStruct((n, dc), BF16),
            jax.ShapeDtypeStruct(s0.shape, F32),
        ],
        compiler_params=_cparams("parallel"),
        name="hgrn_sample",
    )(proj, proj, proj, proj, s0, lb, gn)


def _mix_ffn_kernel(h_ref, ya_ref, yb_ref, woa_ref, wob_ref, g_ref, wg_ref, wu_ref, wd_ref, o_ref,
                    h1_ref, xn_ref, acc_ref):
    k = pl.program_id(1)

    @pl.when(k == 0)
    def _():
        h1 = (h_ref[...] + jnp.dot(ya_ref[...], woa_ref[...], preferred_element_type=F32)
              + jnp.dot(yb_ref[...], wob_ref[...], preferred_element_type=F32))
        h1_ref[...] = h1
        xn_ref[...] = _rms(h1, g_ref[...]).astype(BF16)
        acc_ref[...] = jnp.zeros(acc_ref.shape, F32)

    xn = xn_ref[...]
    a = jnp.dot(xn, wg_ref[...], preferred_element_type=F32)
    b = jnp.dot(xn, wu_ref[...], preferred_element_type=F32)
    acc_ref[...] += jnp.dot((_silu(a) * b).astype(BF16), wd_ref[...], preferred_element_type=F32)

    @pl.when(k == pl.num_programs(1) - 1)
    def _():
        o_ref[...] = h1_ref[...] + acc_ref[...]


def _mix_ffn(h, ya, yb, woa, wob, g, wg, wu, wd):
    n, d = h.shape
    tm = min(512, n)
    tf = D_FF_PAD // 2
    row = lambda w: pl.BlockSpec((tm, w), lambda i, k: (i, 0))
    return pl.pallas_call(
        _mix_ffn_kernel,
        grid=(n // tm, D_FF_PAD // tf),
        in_specs=[
            row(d), row(D_A), row(D_B),
            pl.BlockSpec((D_A, d), lambda i, k: (0, 0)),
            pl.BlockSpec((D_B, d), lambda i, k: (0, 0)),
            pl.BlockSpec((1, d), lambda i, k: (0, 0)),
            pl.BlockSpec((d, tf), lambda i, k: (0, k)),
            pl.BlockSpec((d, tf), lambda i, k: (0, k)),
            pl.BlockSpec((tf, d), lambda i, k: (k, 0)),
        ],
        out_specs=row(d),
        out_shape=jax.ShapeDtypeStruct((n, d), F32),
        scratch_shapes=[pltpu.VMEM((tm, d), F32), pltpu.VMEM((tm, d), BF16), pltpu.VMEM((tm, d), F32)],
        compiler_params=_cparams("parallel", "arbitrary"),
        name="mix_ffn",
    )(h, ya, yb, woa, wob, g, wg, wu, wd)


def _mix_router_kernel(h_ref, y_ref, wo_ref, g_ref, rh_ref, rl_ref, *rest):
    h1_ref, xn_ref, ids_ref, gts_ref = rest[-4:]
    h1 = h_ref[...] + jnp.dot(y_ref[...], wo_ref[...], preferred_element_type=F32)
    h1_ref[...] = h1
    xn = _rms(h1, g_ref[...])
    xn_ref[...] = xn
    xh = xn.astype(BF16)
    xl = (xn - xh.astype(F32)).astype(BF16)
    dot = lambda a, b: jnp.dot(a, b, preferred_element_type=F32)
    logits = dot(xh, rh_ref[...]) + (dot(xh, rl_ref[...]) + dot(xl, rh_ref[...]))
    lane = lax.broadcasted_iota(I32, logits.shape, 1)
    neg = jnp.float32(-jnp.inf)
    logits = jnp.where(lane < N_EXP, logits, neg)
    m1 = jnp.max(logits, axis=-1, keepdims=True)
    i1 = jnp.min(jnp.where(logits == m1, lane, LANES), axis=-1, keepdims=True)
    rest = jnp.where(lane == i1, neg, logits)
    m2 = jnp.max(rest, axis=-1, keepdims=True)
    i2 = jnp.min(jnp.where(rest == m2, lane, LANES), axis=-1, keepdims=True)
    e2 = jnp.exp(m2 - m1)
    g1 = 1.0 / (1.0 + e2)
    g2 = e2 / (1.0 + e2)
    ids_ref[...] = jnp.where(lane == 0, i1, jnp.where(lane == 1, i2, 0))
    gts_ref[...] = jnp.where(lane == 0, g1, jnp.where(lane == 1, g2, 0.0))


def _mix_router(h, y, wo, g, r_hi, r_lo, xn_all, row0):
    n, d = h.shape
    tm = min(512, n)
    row = lambda w: pl.BlockSpec((tm, w), lambda i: (i, 0))
    return pl.pallas_call(
        _mix_router_kernel,
        grid=(n // tm,),
        in_specs=[row(d), row(d),
                  pl.BlockSpec((d, d), lambda i: (0, 0)),
                  pl.BlockSpec((1, d), lambda i: (0, 0)),
                  pl.BlockSpec((d, LANES), lambda i: (0, 0)),
                  pl.BlockSpec((d, LANES), lambda i: (0, 0)),
                  pl.BlockSpec(memory_space=pl.ANY)],
        out_specs=[row(d), pl.BlockSpec((tm, d), lambda i: (i + row0 // tm, 0)), row(LANES), row(LANES)],
        out_shape=[
            jax.ShapeDtypeStruct((n, d), F32),
            jax.ShapeDtypeStruct(xn_all.shape, F32),
            jax.ShapeDtypeStruct((n, LANES), I32),
            jax.ShapeDtypeStruct((n, LANES), F32),
        ],
        input_output_aliases={6: 1},
        compiler_params=_cparams("parallel"),
        name="mix_router",
    )(h, y, wo, g, r_hi, r_lo, xn_all)


def _expert_kernel(te_ref, nu_ref, src_ref, dst_ref, x_hbm, rg_ref, wg_ref, wu_ref, wd_ref, y_hbm,
                   xbuf, xb, ybuf, acc_ref, gsem, ssem, *, tm, nk):
    s = pl.program_id(0)
    k = pl.program_id(1)
    rows = tm // nk
    cur = s % 2
    prev = 1 - cur

    def gather_wait(slot):
        pltpu.make_async_copy(x_hbm.at[pl.ds(0, tm), :], xbuf.at[slot], gsem.at[slot]).wait()

    def scatter_wait(slot):
        pltpu.make_async_copy(ybuf.at[slot], y_hbm.at[pl.ds(0, tm), :], ssem.at[slot]).wait()

    def issue():
        base = s * tm + k * rows
        for j in range(rows):
            r = k * rows + j
            pltpu.make_async_copy(x_hbm.at[pl.ds(src_ref[base + j], 1), :],
                                  xbuf.at[cur, pl.ds(r, 1), :], gsem.at[cur]).start()
            pltpu.make_async_copy(ybuf.at[cur, pl.ds(r, 1), :],
                                  y_hbm.at[pl.ds(dst_ref[base + j], 1), :], ssem.at[cur]).start()

    @pl.when(k == 0)
    def _():
        acc_ref[...] = jnp.zeros(acc_ref.shape, F32)

        @pl.when(s == 0)
        def _():
            ybuf[...] = jnp.zeros(ybuf.shape, F32)

        @pl.when(s > 0)
        def _():
            gather_wait(prev)
            xb[...] = xbuf[prev].astype(BF16)

    valid = (s >= 1) & (s <= nu_ref[0])

    @pl.when(valid)
    def _():
        issue()
        x = xb[...]
        a = jnp.dot(x, wg_ref[0], preferred_element_type=F32)
        b = jnp.dot(x, wu_ref[0], preferred_element_type=F32)
        acc_ref[...] += jnp.dot((_silu(a) * b).astype(BF16), wd_ref[0], preferred_element_type=F32)

    @pl.when(jnp.logical_not(valid))
    def _():
        issue()

    @pl.when(k == nk - 1)
    def _():
        @pl.when(s > 0)
        def _():
            scatter_wait(prev)

        ybuf[prev] = acc_ref[...] * rg_ref[...]

        @pl.when(s == pl.num_programs(0) - 1)
        def _():
            scatter_wait(cur)
            gather_wait(cur)


def _experts(x_all, src_step, dst_step, row_gate, step_expert, n_used, wg, wu, wd, n_y):
    tm = MOE_TM
    nk = 4
    tk = D_EXP // nk
    n_steps = src_step.shape[0] // tm
    n_tiles = n_steps - 2
    kern = functools.partial(_expert_kernel, tm=tm, nk=nk)

    def kk(s, k, nu):
        return jnp.where((s >= 1) & (s <= nu[0]), k, nk - 1)

    def tile(s):
        return jnp.clip(s - 1, 0, n_tiles - 1)

    grid_spec = pltpu.PrefetchScalarGridSpec(
        num_scalar_prefetch=4,
        grid=(n_steps, nk),
        in_specs=[
            pl.BlockSpec(memory_space=pl.ANY),
            pl.BlockSpec((tm, 1), lambda s, k, te, nu, sr, ds: (tile(s), 0)),
            pl.BlockSpec((1, D_MODEL, tk), lambda s, k, te, nu, sr, ds: (te[s], 0, kk(s, k, nu))),
            pl.BlockSpec((1, D_MODEL, tk), lambda s, k, te, nu, sr, ds: (te[s], 0, kk(s, k, nu))),
            pl.BlockSpec((1, tk, D_MODEL), lambda s, k, te, nu, sr, ds: (te[s], kk(s, k, nu), 0)),
        ],
        out_specs=pl.BlockSpec(memory_space=pl.ANY),
        scratch_shapes=[
            pltpu.VMEM((2, tm, D_MODEL), F32), pltpu.VMEM((tm, D_MODEL), BF16),
            pltpu.VMEM((2, tm, D_MODEL), F32), pltpu.VMEM((tm, D_MODEL), F32),
            pltpu.SemaphoreType.DMA((2,)), pltpu.SemaphoreType.DMA((2,)),
        ],
    )
    return pl.pallas_call(
        kern,
        grid_spec=grid_spec,
        out_shape=jax.ShapeDtypeStruct((n_y, D_MODEL), F32),
        compiler_params=_cparams("arbitrary", "arbitrary"),
        name="moe_experts",
    )(step_expert, n_used, src_step, dst_step, x_all, row_gate, wg, wu, wd)


def _add3_kernel(h_ref, y0_ref, y1_ref, o_ref):
    o_ref[...] = h_ref[...] + (y0_ref[...] + y1_ref[...])


def _add3(h1, y, row0, row1):
    n, d = h1.shape
    tt = min(512, n)
    return pl.pallas_call(
        _add3_kernel,
        grid=(n // tt,),
        in_specs=[pl.BlockSpec((tt, d), lambda i: (i, 0)),
                  pl.BlockSpec((tt, d), lambda i: (i + row0 // tt, 0)),
                  pl.BlockSpec((tt, d), lambda i: (i + row1 // tt, 0))],
        out_specs=pl.BlockSpec((tt, d), lambda i: (i, 0)),
        out_shape=jax.ShapeDtypeStruct((n, d), F32),
        compiler_params=_cparams("parallel"),
        name="moe_add",
    )(h1, y, y)


def _moe(h1_p, h1_s, x_all, ids_p, ids_s, gts_p, gts_s, wg, wu, wd):
    n_p, n_s = h1_p.shape[0], h1_s.shape[0]
    n_tok = n_p + n_s
    tm = MOE_TM
    n_tiles = (2 * n_tok + N_EXP * (tm - 1)) // tm
    n_rows = n_tiles * tm

    e = jnp.concatenate([ids_p[:, :2], ids_s[:, :2]], axis=0)
    gt = jnp.concatenate([gts_p[:, :2], gts_s[:, :2]], axis=0)
    hit = jnp.sum((e[:, :, None] == jnp.arange(N_EXP, dtype=I32)).astype(I32), axis=1)
    cum = jnp.cumsum(hit, axis=0)
    rank = cum - hit
    counts = cum[-1]
    padded = ((counts + tm - 1) // tm) * tm
    ends = jnp.cumsum(padded)
    offs = ends - padded
    pos = (offs[e] + jnp.take_along_axis(rank, e, axis=1)).reshape(-1)
    n_used = (ends[-1] // tm).astype(I32).reshape(1)

    tok = jnp.arange(n_tok, dtype=I32)[:, None]
    slot = jnp.arange(2, dtype=I32)[None, :]
    dest = jnp.where(tok < n_p, slot * n_p + tok, 2 * n_p + slot * n_s + (tok - n_p)).reshape(-1)
    n_y = 2 * n_tok + 2 * tm

    row_tok = jnp.zeros((n_rows,), I32).at[pos].set(jnp.broadcast_to(tok, (n_tok, 2)).reshape(-1),
                                                    unique_indices=True)
    row_dst = jnp.full((n_rows,), -1, I32).at[pos].set(dest, unique_indices=True)
    row_gate = jnp.zeros((n_rows,), F32).at[pos].set(gt.reshape(-1), unique_indices=True)

    n_steps = n_tiles + 2
    src_step = jnp.concatenate([row_tok, jnp.zeros((2 * tm,), I32)])
    dst_tile = jnp.concatenate([jnp.full((2 * tm,), -1, I32), row_dst]).reshape(n_steps, tm)
    spare = (2 * n_tok + (jnp.arange(n_steps, dtype=I32)[:, None] % 2) * tm + jnp.arange(tm, dtype=I32)[None, :])
    dst_step = jnp.where(dst_tile >= 0, dst_tile, spare).reshape(-1)
    tile_of_step = jnp.clip(jnp.arange(n_steps, dtype=I32) - 1, 0, n_used[0] - 1)
    step_expert = jnp.sum((ends[None, :] <= (tile_of_step * tm)[:, None]).astype(I32), axis=1)
    step_expert = jnp.minimum(step_expert, N_EXP - 1).astype(I32)

    y = _experts(x_all, src_step, dst_step, row_gate.reshape(n_rows, 1), step_expert, n_used, wg, wu, wd, n_y)
    out_p = _add3(h1_p, y, 0, n_p)
    out_s = _add3(h1_s, y, 2 * n_p, 2 * n_p + n_s)
    return out_p, out_s


def _final_norm_kernel(x_ref, g_ref, o_ref):
    o_ref[...] = _rms(x_ref[...], g_ref[...])


def _final_norm(x, g):
    n, d = x.shape
    tm = min(1024, n)
    return pl.pallas_call(
        _final_norm_kernel,
        grid=(n // tm,),
        in_specs=[pl.BlockSpec((tm, d), lambda i: (i, 0)), pl.BlockSpec((1, d), lambda i: (0, 0))],
        out_specs=pl.BlockSpec((tm, d), lambda i: (i, 0)),
        out_shape=jax.ShapeDtypeStruct((n, d), F32),
        compiler_params=_cparams("parallel"),
        name="final_norm",
    )(x, g)


def kernel(x_prompt, x_sample, state_conv, state_ssm_re, state_ssm_im, state_hgrn, norm_mix, norm_ffn, norm_final, ab_w_in, ab_w_out, conv_w, conv_b, conv_ln_g, conv_ln_b, ssm_a_re, ssm_a_im, ssm_log_dt, ssm_b_re, ssm_b_im, ssm_c_re, ssm_c_im, ssm_d, ssm_glu_w, ssm_glu_b, hg_w_in, hg_lb_logits, hg_gnorm, hg_w_out, ffn_w_gate, ffn_w_up, ffn_w_down, moe_router, moe_w_gate, moe_w_up, moe_w_down):
    bsz, seq, d = x_prompt.shape
    n_s = x_sample.shape[0]
    depth = norm_mix.shape[0]
    hp = x_prompt.reshape(bsz * seq, d)
    hs = x_sample.reshape(n_s, d)

    lbs = jax.nn.softmax(hg_lb_logits.astype(F32), axis=0)
    lbs = jnp.cumsum(lbs, axis=0) - lbs[0:1]

    conv_p, re_p, im_p, hg_p = [], [], [], []
    conv_s, re_s, im_s, hg_s = [], [], [], []
    zero_state = jnp.zeros((bsz, S5_STATE), F32)
    x_all = jnp.zeros((bsz * seq + n_s, d), F32)
    row = lambda v: v.reshape(1, -1).astype(F32)

    for l in range(depth):
        j = l // 2
        g_mix, g_ffn = row(norm_mix[l]), row(norm_ffn[l])
        if l % 2 == 0:
            w_in = ab_w_in[j].astype(BF16)
            woa = ab_w_out[j, :D_A].astype(BF16)
            wob = ab_w_out[j, D_A:].astype(BF16)
            cw, cb, clg, clb = conv_w[j].astype(F32), row(conv_b[j]), row(conv_ln_g[j]), row(conv_ln_b[j])
            abar_re, abar_im, bcat, ccat = _s5_params(
                ssm_a_re[j].astype(F32), ssm_a_im[j].astype(F32), ssm_log_dt[j].astype(F32),
                ssm_b_re[j].astype(F32), ssm_b_im[j].astype(F32), ssm_c_re[j].astype(F32), ssm_c_im[j].astype(F32))
            sd, sgw, sgb = row(ssm_d[j]), ssm_glu_w[j].astype(BF16), row(ssm_glu_b[j])
            pad = ((0, 0), (0, D_FF_PAD - D_FF))
            wg = jnp.pad(ffn_w_gate[j], pad).astype(BF16)
            wu = jnp.pad(ffn_w_up[j], pad).astype(BF16)
            wd = jnp.pad(ffn_w_down[j], (pad[1], pad[0])).astype(BF16)

            proj = _norm_matmul(hp, g_mix, w_in, tn=w_in.shape[1], out_dtype=BF16)
            proj3 = proj.reshape(bsz, seq, -1)
            ya, cb_new = _conv_prompt(proj3, cw, cb, clg, clb)
            yb, hr, hi = _s5(proj3, zero_state, zero_state, abar_re, abar_im, bcat, ccat, sd, sgw, sgb,
                             tt=min(128, seq), seq_major=True)
            hp = _mix_ffn(hp, ya.reshape(bsz * seq, D_A), yb.reshape(bsz * seq, D_B), woa, wob, g_ffn,
                          wg, wu, wd)
            conv_p.append(cb_new)
            re_p.append(hr.reshape(bsz, S5_G, S5_P))
            im_p.append(hi.reshape(bsz, S5_G, S5_P))

            proj = _norm_matmul(hs, g_mix, w_in, tn=w_in.shape[1])
            ya, nb_t = _conv_sample(proj, jnp.transpose(state_conv[j].astype(F32), (1, 0, 2)), cw, cb, clg, clb)
            yb_t, hr, hi = _s5(proj[:, 2 * D_A:].reshape(1, n_s, D_B),
                               state_ssm_re[j].astype(F32).reshape(n_s, S5_STATE),
                               state_ssm_im[j].astype(F32).reshape(n_s, S5_STATE),
                               abar_re, abar_im, bcat, ccat, sd, sgw, sgb, tt=1, seq_major=False)
            hs = _mix_ffn(hs, ya, yb_t.reshape(n_s, D_B), woa, wob, g_ffn, wg, wu, wd)
            conv_s.append(jnp.transpose(nb_t, (1, 0, 2)))
            re_s.append(hr.reshape(n_s, S5_G, S5_P))
            im_s.append(hi.reshape(n_s, S5_G, S5_P))
        else:
            w_in = hg_w_in[j].astype(BF16)
            wo = hg_w_out[j].astype(BF16)
            lb, gn = row(lbs[j]), row(hg_gnorm[j])
            router_pad = jnp.pad(moe_router[j].astype(F32), ((0, 0), (0, LANES - N_EXP)))
            r_hi = router_pad.astype(BF16)
            r_lo = (router_pad - r_hi.astype(F32)).astype(BF16)
            wg, wu, wd = moe_w_gate[j].astype(BF16), moe_w_up[j].astype(BF16), moe_w_down[j].astype(BF16)

            proj = _norm_matmul(hp, g_mix, w_in, tn=1024, out_dtype=BF16)
            o, s_new = _hgrn_prompt(proj.reshape(bsz, seq, -1), lb, gn)
            h1_p, x_all, ids_p, gts_p = _mix_router(hp, o.reshape(bsz * seq, d), wo, g_ffn, r_hi, r_lo,
                                                    x_all, 0)
            hg_p.append(s_new)

            proj = _norm_matmul(hs, g_mix, w_in, tn=1024)
            o, s_new = _hgrn_sample(proj, state_hgrn[j].astype(F32), lb, gn)
            h1_s, x_all, ids_s, gts_s = _mix_router(hs, o, wo, g_ffn, r_hi, r_lo, x_all, bsz * seq)
            hg_s.append(s_new)

            hp, hs = _moe(h1_p, h1_s, x_all, ids_p, ids_s, gts_p, gts_s, wg, wu, wd)

    g_fin = row(norm_final)
    y_prompt = _final_norm(hp, g_fin).reshape(bsz, seq, d)
    y_sample = _final_norm(hs, g_fin).reshape(n_s, 1, d)
    return (y_prompt, y_sample, jnp.stack(conv_p), jnp.stack(re_p), jnp.stack(im_p), jnp.stack(hg_p),
            jnp.stack(conv_s), jnp.stack(re_s), jnp.stack(im_s), jnp.stack(hg_s))
```

```python
import functools
import math

import jax
import jax.numpy as jnp
from jax import lax
from jax.experimental import pallas as pl
from jax.experimental.pallas import tpu as pltpu

F32 = jnp.float32
BF16 = jnp.bfloat16
I32 = jnp.int32

EPS = 1e-6
LN_EPS = 1e-5
FORGET_FLOOR = 1e-30

D_MODEL = 1024
D_A = 512
D_B = 512
CONV_W = 31
CONV_HALO = 32
S5_G = 32
S5_P = 64
S5_GH = 16
S5_STATE = S5_G * S5_P
S5_CHUNKS = 4
HG_HEADS = 8
HG_K = 128
HG_V = 128
HG_CHUNK = 64
D_FF = 2752
D_FF_PAD = 2816
N_EXP = 8
D_EXP = 3584
LANES = 128
VMEM_LIMIT_BYTES = 56 * 1024 * 1024
EXP_CLAMP = 80.0

MOE_TM = 512


def _cparams(*sem):
    return pltpu.CompilerParams(dimension_semantics=sem, vmem_limit_bytes=VMEM_LIMIT_BYTES)


def _sigmoid(x):
    return 1.0 / (1.0 + jnp.exp(-x))


def _silu(x):
    return x * _sigmoid(x)


def _rms(x, g):
    return x * lax.rsqrt(jnp.mean(x * x, axis=-1, keepdims=True) + EPS) * g


def _bdot(a, b):
    return jnp.dot(a.astype(BF16), b.astype(BF16), preferred_element_type=F32)


def _norm_matmul_kernel(x_ref, g_ref, w_ref, o_ref, xn_ref):
    @pl.when(pl.program_id(1) == 0)
    def _():
        xn_ref[...] = _rms(x_ref[...], g_ref[...]).astype(BF16)

    o_ref[...] = jnp.dot(xn_ref[...], w_ref[...], preferred_element_type=F32).astype(o_ref.dtype)


def _norm_matmul(x, g, w, tn, out_dtype=F32):
    n, d = x.shape
    nout = w.shape[1]
    tm = min(1024, n)
    return pl.pallas_call(
        _norm_matmul_kernel,
        grid=(n // tm, nout // tn),
        in_specs=[
            pl.BlockSpec((tm, d), lambda i, j: (i, 0)),
            pl.BlockSpec((1, d), lambda i, j: (0, 0)),
            pl.BlockSpec((d, tn), lambda i, j: (0, j)),
        ],
        out_specs=pl.BlockSpec((tm, tn), lambda i, j: (i, j)),
        out_shape=jax.ShapeDtypeStruct((n, nout), out_dtype),
        scratch_shapes=[pltpu.VMEM((tm, d), BF16)],
        compiler_params=_cparams("parallel", "arbitrary"),
        name="norm_matmul",
    )(x, g, w)


def _conv_post(y, b, lg, lb):
    y = y + b
    mu = jnp.mean(y, axis=-1, keepdims=True)
    yc = y - mu
    var = jnp.mean(yc * yc, axis=-1, keepdims=True)
    return _silu(yc * lax.rsqrt(var + LN_EPS) * lg + lb)


def _conv_prompt_kernel(av_ref, ag_ref, w_ref, b_ref, lg_ref, lb_ref, y_ref, nb_ref, ubuf, *, tt, rb):
    t = pl.program_id(1)

    @pl.when(t == 0)
    def _():
        ubuf[0:CONV_HALO, :] = jnp.zeros((CONV_HALO, D_A), F32)

    @pl.when(t > 0)
    def _():
        ubuf[0:CONV_HALO, :] = ubuf[tt:tt + CONV_HALO, :]

    ubuf[CONV_HALO:CONV_HALO + tt, :] = av_ref[0].astype(F32) * _sigmoid(ag_ref[0].astype(F32))
    off = CONV_HALO - (CONV_W - 1)
    span = rb + CONV_HALO
    for r in range(tt // rb):
        win = ubuf[r * rb:r * rb + span, :]
        acc = jnp.zeros((rb, D_A), F32)
        for p in range(8):
            wp = win if p == 0 else pltpu.roll(win, span - p, 0)
            for q in range(CONV_HALO // 8 + 1):
                k = 8 * q + p - off
                if 0 <= k < CONV_W:
                    acc = acc + w_ref[k:k + 1, :] * wp[8 * q:8 * q + rb, :]
        y_ref[0, r * rb:(r + 1) * rb, :] = _conv_post(acc, b_ref[...], lg_ref[...], lb_ref[...]).astype(y_ref.dtype)

    @pl.when(t == pl.num_programs(1) - 1)
    def _():
        nb_ref[0] = ubuf[tt + off:tt + CONV_HALO, :]


def _conv_prompt(proj3, w, b, lg, lb):
    bsz, t, _ = proj3.shape
    tt = min(256, t)
    kern = functools.partial(_conv_prompt_kernel, tt=tt, rb=32)
    vec = pl.BlockSpec((1, D_A), lambda i, j: (0, 0))
    return pl.pallas_call(
        kern,
        grid=(bsz, t // tt),
        in_specs=[
            pl.BlockSpec((1, tt, D_A), lambda i, j: (i, j, 0)),
            pl.BlockSpec((1, tt, D_A), lambda i, j: (i, j, 1)),
            pl.BlockSpec((CONV_W, D_A), lambda i, j: (0, 0)),
            vec, vec, vec,
        ],
        out_specs=[
            pl.BlockSpec((1, tt, D_A), lambda i, j: (i, j, 0)),
            pl.BlockSpec((1, CONV_W - 1, D_A), lambda i, j: (i, 0, 0)),
        ],
        out_shape=[
            jax.ShapeDtypeStruct((bsz, t, D_A), BF16),
            jax.ShapeDtypeStruct((bsz, CONV_W - 1, D_A), F32),
        ],
        scratch_shapes=[pltpu.VMEM((CONV_HALO + tt, D_A), F32)],
        compiler_params=_cparams("parallel", "arbitrary"),
        name="conv_prompt",
    )(proj3, proj3, w, b, lg, lb)


def _conv_sample_kernel(av_ref, ag_ref, buf_ref, w_ref, b_ref, lg_ref, lb_ref, y_ref, nb_ref):
    u = av_ref[...] * _sigmoid(ag_ref[...])
    acc = w_ref[CONV_W - 1:CONV_W, :] * u
    for k in range(CONV_W - 1):
        acc = acc + w_ref[k:k + 1, :] * buf_ref[k]
    y_ref[...] = _conv_post(acc, b_ref[...], lg_ref[...], lb_ref[...]).astype(y_ref.dtype)
    for k in range(CONV_W - 2):
        nb_ref[k] = buf_ref[k + 1]
    nb_ref[CONV_W - 2] = u


def _conv_sample(proj, buf_t, w, b, lg, lb):
    n = proj.shape[0]
    nbk = min(32, n)
    vec = pl.BlockSpec((1, D_A), lambda i: (0, 0))
    return pl.pallas_call(
        _conv_sample_kernel,
        grid=(n // nbk,),
        in_specs=[
            pl.BlockSpec((nbk, D_A), lambda i: (i, 0)),
            pl.BlockSpec((nbk, D_A), lambda i: (i, 1)),
            pl.BlockSpec((CONV_W - 1, nbk, D_A), lambda i: (0, i, 0)),
            pl.BlockSpec((CONV_W, D_A), lambda i: (0, 0)),
            vec, vec, vec,
        ],
        out_specs=[
            pl.BlockSpec((nbk, D_A), lambda i: (i, 0)),
            pl.BlockSpec((CONV_W - 1, nbk, D_A), lambda i: (0, i, 0)),
        ],
        out_shape=[
            jax.ShapeDtypeStruct((n, D_A), BF16),
            jax.ShapeDtypeStruct((CONV_W - 1, n, D_A), F32),
        ],
        compiler_params=_cparams("parallel"),
        name="conv_sample",
    )(proj, proj, buf_t, w, b, lg, lb)


def _s5_kernel(u_ref, h0r_ref, h0i_ref, ar_ref, ai_ref, bcat_ref, ccat_ref, d_ref, gw_ref, gb_ref,
               y_ref, hr_out, hi_out, sre, sim, cre, cim, rt, *, tt, nb, lw, seq_major):
    i = pl.program_id(0)
    m = tt * nb
    cw = S5_STATE // S5_CHUNKS

    @pl.when(i == 0)
    def _():
        cre[...] = h0r_ref[...]
        cim[...] = h0i_ref[...]

    if seq_major:
        for b in range(nb):
            rt[:, b, :] = u_ref[b].astype(F32)
        u = rt[...].reshape(m, D_B)
    else:
        u = u_ref[...].reshape(m, D_B)
    ub = u.astype(BF16)
    for c in range(S5_CHUNKS):
        bu = jnp.dot(ub[:, c * LANES:(c + 1) * LANES], bcat_ref[c], preferred_element_type=F32)
        sre[:, :, c * cw:(c + 1) * cw] = bu[:, :cw].reshape(tt, nb, cw)
        sim[:, :, c * cw:(c + 1) * cw] = bu[:, cw:].reshape(tt, nb, cw)

    for c in range(S5_STATE // lw):
        ls = slice(c * lw, (c + 1) * lw)
        ar = jnp.broadcast_to(ar_ref[:, ls], (nb, lw))
        ai = jnp.broadcast_to(ai_ref[:, ls], (nb, lw))

        def body(t, carry, ls=ls, ar=ar, ai=ai):
            hr, hi = carry
            nr = ar * hr - ai * hi + sre[t, :, ls]
            ni = ar * hi + ai * hr + sim[t, :, ls]
            sre[t, :, ls] = nr
            sim[t, :, ls] = ni
            return nr, ni

        hr, hi = lax.fori_loop(0, tt, body, (cre[:, ls], cim[:, ls]), unroll=min(tt, 8))
        cre[:, ls] = hr
        cim[:, ls] = hi

    hre = sre[...].reshape(m, S5_STATE)
    him = sim[...].reshape(m, S5_STATE)
    ys = []
    for c in range(S5_CHUNKS):
        hcat = jnp.concatenate([hre[:, c * cw:(c + 1) * cw], him[:, c * cw:(c + 1) * cw]], axis=1)
        ys.append(jnp.dot(hcat.astype(BF16), ccat_ref[c], preferred_element_type=F32))
    y = jnp.concatenate(ys, axis=1) + d_ref[...] * u
    y = 0.5 * y * (1.0 + lax.erf(y * (1.0 / math.sqrt(2.0))))
    z = jnp.dot(y.astype(BF16), gw_ref[...], preferred_element_type=F32) + gb_ref[...]
    out = (y * _sigmoid(z)).reshape(tt, nb, D_B)
    if seq_major:
        rt[...] = out
        for b in range(nb):
            y_ref[b] = rt[:, b, :].astype(y_ref.dtype)
    else:
        y_ref[...] = out.astype(y_ref.dtype)

    @pl.when(i == pl.num_programs(0) - 1)
    def _():
        hr_out[...] = cre[...]
        hi_out[...] = cim[...]


def _s5(u, h0r, h0i, abar_re, abar_im, bcat, ccat, d, glu_w, glu_b, tt, seq_major):
    if seq_major:
        nb, t, c = u.shape
        u_spec = pl.BlockSpec((nb, tt, D_B), lambda i: (0, i, c // D_B - 1))
        y_spec = pl.BlockSpec((nb, tt, D_B), lambda i: (0, i, 0))
        y_shape = (nb, t, D_B)
    else:
        t, nb, _ = u.shape
        u_spec = pl.BlockSpec((tt, nb, D_B), lambda i: (i, 0, 0))
        y_spec = u_spec
        y_shape = (t, nb, D_B)
    lw = max(LANES, min(512, 8 * 1024 // nb))
    kern = functools.partial(_s5_kernel, tt=tt, nb=nb, lw=lw, seq_major=seq_major)
    full = lambda shape: pl.BlockSpec(shape, lambda i: (0,) * len(shape))
    return pl.pallas_call(
        kern,
        grid=(t // tt,),
        in_specs=[
            u_spec,
            full((nb, S5_STATE)), full((nb, S5_STATE)),
            full((1, S5_STATE)), full((1, S5_STATE)),
            full(bcat.shape), full(ccat.shape),
            full((1, D_B)), full((D_B, D_B)), full((1, D_B)),
        ],
        out_specs=[
            y_spec,
            full((nb, S5_STATE)), full((nb, S5_STATE)),
        ],
        out_shape=[
            jax.ShapeDtypeStruct(y_shape, BF16),
            jax.ShapeDtypeStruct((nb, S5_STATE), F32),
            jax.ShapeDtypeStruct((nb, S5_STATE), F32),
        ],
        scratch_shapes=[
            pltpu.VMEM((tt, nb, S5_STATE), F32), pltpu.VMEM((tt, nb, S5_STATE), F32),
            pltpu.VMEM((nb, S5_STATE), F32), pltpu.VMEM((nb, S5_STATE), F32),
            pltpu.VMEM((tt, nb, D_B), F32),
        ],
        compiler_params=_cparams("arbitrary"),
        name="s5",
    )(u, h0r, h0i, abar_re, abar_im, bcat, ccat, d, glu_w, glu_b)


def _s5_params(a_re, a_im, log_dt, b_re, b_im, c_re, c_im):
    dt = jnp.exp(log_dt)[:, None]
    mag = jnp.exp(dt * a_re)
    ang = dt * a_im
    abar_re, abar_im = mag * jnp.cos(ang), mag * jnp.sin(ang)
    den = a_re * a_re + a_im * a_im
    nr, ni = abar_re - 1.0, abar_im
    coef_re = (nr * a_re + ni * a_im) / den
    coef_im = (ni * a_re - nr * a_im) / den
    bbar_re = coef_re[..., None] * b_re - coef_im[..., None] * b_im
    bbar_im = coef_re[..., None] * b_im + coef_im[..., None] * b_re
    gpc = S5_G // S5_CHUNKS
    eye = jnp.eye(gpc, dtype=F32)

    def bblk(x):
        x = x.reshape(S5_CHUNKS, gpc, S5_P, S5_GH)
        return jnp.einsum("cgph,gk->cghkp", x, eye).reshape(S5_CHUNKS, gpc * S5_GH, gpc * S5_P)

    def cblk(x):
        x = x.reshape(S5_CHUNKS, gpc, S5_GH, S5_P)
        return jnp.einsum("cghp,gk->cgpkh", x, eye).reshape(S5_CHUNKS, gpc * S5_P, gpc * S5_GH)

    bcat = jnp.concatenate([bblk(bbar_re), bblk(bbar_im)], axis=2).astype(BF16)
    ccat = jnp.concatenate([cblk(c_re), -cblk(c_im)], axis=1).astype(BF16)
    return abar_re.reshape(1, S5_STATE), abar_im.reshape(1, S5_STATE), bcat, ccat


def _hg_gates(q, f, lb):
    qf = _silu(q)
    sig = _sigmoid(f)
    forget = lb + (1.0 - lb) * sig
    logg = jnp.log(jnp.maximum(forget, FORGET_FLOOR))
    kf = (1.0 - lb) * (1.0 - sig)
    return qf, logg, kf


def _hg_out(o, g, gn):
    return o * lax.rsqrt(jnp.mean(o * o, axis=-1, keepdims=True) + EPS) * gn * _silu(g)


def _cumsum_rows(x, tri):
    hi = x.astype(BF16)
    r1 = x - hi.astype(F32)
    mid = r1.astype(BF16)
    lo = (r1 - mid.astype(F32)).astype(BF16)
    dot = lambda p: jnp.dot(tri, p, preferred_element_type=F32)
    return dot(hi) + dot(mid) + dot(lo)


def _hgrn_prompt_kernel(q_ref, f_ref, v_ref, g_ref, lb_ref, gn_ref, o_ref, s_out, st, *, tt):
    t = pl.program_id(1)
    L = HG_CHUNK

    @pl.when(t == 0)
    def _():
        st[...] = jnp.zeros(st.shape, F32)

    row = lax.broadcasted_iota(I32, (L, L), 0)
    col = lax.broadcasted_iota(I32, (L, L), 1)
    causal = row >= col
    tri = causal.astype(BF16)
    nt = (((1,), (1,)), ((), ()))
    tn = (((0,), (0,)), ((), ()))
    for c in range(tt // L):
        rs = slice(c * L, (c + 1) * L)
        q, logg, k = _hg_gates(q_ref[0, rs, :].astype(F32), f_ref[0, rs, :].astype(F32), lb_ref[...])
        vb = v_ref[0, rs, :].astype(BF16)
        gc = _cumsum_rows(logg, tri)
        gmid = gc[L // 2 - 1:L // 2, :]
        glast = gc[L - 1:L, :]
        qe = (q * jnp.exp(jnp.minimum(gc - gmid, EXP_CLAMP))).astype(BF16)
        ke = (k * jnp.exp(jnp.minimum(gmid - gc, EXP_CLAMP))).astype(BF16)
        qg = (q * jnp.exp(gc)).astype(BF16)
        kd = (k * jnp.exp(glast - gc)).astype(BF16)
        dlast = jnp.exp(glast)
        for h in range(HG_HEADS):
            hs = slice(h * HG_K, (h + 1) * HG_K)
            sc = lax.dot_general(qe[:, hs], ke[:, hs], nt, preferred_element_type=F32)
            sc = jnp.where(causal, sc, 0.0).astype(BF16)
            s_t = st[h]
            o = (jnp.dot(sc, vb[:, hs], preferred_element_type=F32)
                 + lax.dot_general(qg[:, hs], s_t.astype(BF16), nt, preferred_element_type=F32))
            st[h] = dlast[:, hs] * s_t + lax.dot_general(vb[:, hs], kd[:, hs], tn, preferred_element_type=F32)
            o_ref[0, rs, hs] = _hg_out(o, g_ref[0, rs, hs].astype(F32), gn_ref[...]).astype(o_ref.dtype)

    @pl.when(t == pl.num_programs(1) - 1)
    def _():
        for h in range(HG_HEADS):
            s_out[0, h] = st[h].T


def _hgrn_prompt(proj3, lb, gn):
    bsz, t, _ = proj3.shape
    tt = min(256, t)
    dc = HG_HEADS * HG_K
    kern = functools.partial(_hgrn_prompt_kernel, tt=tt)
    blk = lambda c: pl.BlockSpec((1, tt, dc), lambda i, j, c=c: (i, j, c))
    return pl.pallas_call(
        kern,
        grid=(bsz, t // tt),
        in_specs=[blk(0), blk(1), blk(2), blk(3),
                  pl.BlockSpec((1, dc), lambda i, j: (0, 0)),
                  pl.BlockSpec((1, HG_V), lambda i, j: (0, 0))],
        out_specs=[
            pl.BlockSpec((1, tt, dc), lambda i, j: (i, j, 0)),
            pl.BlockSpec((1, HG_HEADS, HG_K, HG_V), lambda i, j: (i, 0, 0, 0)),
        ],
        out_shape=[
            jax.ShapeDtypeStruct((bsz, t, dc), BF16),
            jax.ShapeDtypeStruct((bsz, HG_HEADS, HG_K, HG_V), F32),
        ],
        scratch_shapes=[pltpu.VMEM((HG_HEADS, HG_V, HG_K), F32)],
        compiler_params=_cparams("parallel", "arbitrary"),
        name="hgrn_prompt",
    )(proj3, proj3, proj3, proj3, lb, gn)


def _hgrn_sample_kernel(q_ref, f_ref, v_ref, g_ref, s_ref, lb_ref, gn_ref, o_ref, s_out, *, nbk):
    for h in range(HG_HEADS):
        hs = slice(h * HG_K, (h + 1) * HG_K)
        q, logg, k = _hg_gates(q_ref[:, hs], f_ref[:, hs], lb_ref[:, hs])
        dec = jnp.exp(logg)
        v = v_ref[:, hs]
        rows = []
        for n in range(nbk):
            col = lambda x: jnp.broadcast_to(x[n:n + 1, :], (HG_K, HG_K)).T
            s_new = col(dec) * s_ref[n, h] + col(k) * v[n:n + 1, :]
            s_out[n, h] = s_new
            rows.append(jnp.sum(col(q) * s_new, axis=0, keepdims=True))
        o = jnp.concatenate(rows, axis=0)
        o_ref[:, hs] = _hg_out(o, g_ref[:, hs], gn_ref[...]).astype(o_ref.dtype)


def _hgrn_sample(proj, s0, lb, gn):
    n = proj.shape[0]
    nbk = 8
    dc = HG_HEADS * HG_K
    kern = functools.partial(_hgrn_sample_kernel, nbk=nbk)
    blk = lambda c: pl.BlockSpec((nbk, dc), lambda i, c=c: (i, c))
    sblk = pl.BlockSpec((nbk, HG_HEADS, HG_K, HG_V), lambda i: (i, 0, 0, 0))
    return pl.pallas_call(
        kern,
        grid=(n // nbk,),
        in_specs=[blk(0), blk(1), blk(2), blk(3), sblk,
                  pl.BlockSpec((1, dc), lambda i: (0, 0)),
                  pl.BlockSpec((1, HG_V), lambda i: (0, 0))],
        out_specs=[pl.BlockSpec((nbk, dc), lambda i: (i, 0)), sblk],
        out_shape=[
            jax.ShapeDtypeStruct((n, dc), BF16),
            jax.ShapeDtypeStruct(s0.shape, F32),
        ],
        compiler_params=_cparams("parallel"),
        name="hgrn_sample",
    )(proj, proj, proj, proj, s0, lb, gn)


def _mix_ffn_kernel(h_ref, ya_ref, yb_ref, woa_ref, wob_ref, g_ref, wg_ref, wu_ref, wd_ref, o_ref,
                    h1_ref, xn_ref, acc_ref):
    k = pl.program_id(1)

    @pl.when(k == 0)
    def _():
        h1 = (h_ref[...] + jnp.dot(ya_ref[...], woa_ref[...], preferred_element_type=F32)
              + jnp.dot(yb_ref[...], wob_ref[...], preferred_element_type=F32))
        h1_ref[...] = h1
        xn_ref[...] = _rms(h1, g_ref[...]).astype(BF16)
        acc_ref[...] = jnp.zeros(acc_ref.shape, F32)

    xn = xn_ref[...]
    a = jnp.dot(xn, wg_ref[...], preferred_element_type=F32)
    b = jnp.dot(xn, wu_ref[...], preferred_element_type=F32)
    acc_ref[...] += jnp.dot((_silu(a) * b).astype(BF16), wd_ref[...], preferred_element_type=F32)

    @pl.when(k == pl.num_programs(1) - 1)
    def _():
        o_ref[...] = h1_ref[...] + acc_ref[...]


def _mix_ffn(h, ya, yb, woa, wob, g, wg, wu, wd):
    n, d = h.shape
    tm = min(512, n)
    tf = D_FF_PAD // 2
    row = lambda w: pl.BlockSpec((tm, w), lambda i, k: (i, 0))
    return pl.pallas_call(
        _mix_ffn_kernel,
        grid=(n // tm, D_FF_PAD // tf),
        in_specs=[
            row(d), row(D_A), row(D_B),
            pl.BlockSpec((D_A, d), lambda i, k: (0, 0)),
            pl.BlockSpec((D_B, d), lambda i, k: (0, 0)),
            pl.BlockSpec((1, d), lambda i, k: (0, 0)),
            pl.BlockSpec((d, tf), lambda i, k: (0, k)),
            pl.BlockSpec((d, tf), lambda i, k: (0, k)),
            pl.BlockSpec((tf, d), lambda i, k: (k, 0)),
        ],
        out_specs=row(d),
        out_shape=jax.ShapeDtypeStruct((n, d), F32),
        scratch_shapes=[pltpu.VMEM((tm, d), F32), pltpu.VMEM((tm, d), BF16), pltpu.VMEM((tm, d), F32)],
        compiler_params=_cparams("parallel", "arbitrary"),
        name="mix_ffn",
    )(h, ya, yb, woa, wob, g, wg, wu, wd)


def _mix_router_kernel(h_ref, y_ref, wo_ref, g_ref, rh_ref, rl_ref, *rest):
    h1_ref, xn_ref, ids_ref, gts_ref = rest[-4:]
    h1 = h_ref[...] + jnp.dot(y_ref[...], wo_ref[...], preferred_element_type=F32)
    h1_ref[...] = h1
    xn = _rms(h1, g_ref[...])
    xn_ref[...] = xn
    xh = xn.astype(BF16)
    xl = (xn - xh.astype(F32)).astype(BF16)
    dot = lambda a, b: jnp.dot(a, b, preferred_element_type=F32)
    logits = dot(xh, rh_ref[...]) + (dot(xh, rl_ref[...]) + dot(xl, rh_ref[...]))
    lane = lax.broadcasted_iota(I32, logits.shape, 1)
    neg = jnp.float32(-jnp.inf)
    logits = jnp.where(lane < N_EXP, logits, neg)
    m1 = jnp.max(logits, axis=-1, keepdims=True)
    i1 = jnp.min(jnp.where(logits == m1, lane, LANES), axis=-1, keepdims=True)
    rest = jnp.where(lane == i1, neg, logits)
    m2 = jnp.max(rest, axis=-1, keepdims=True)
    i2 = jnp.min(jnp.where(rest == m2, lane, LANES), axis=-1, keepdims=True)
    e2 = jnp.exp(m2 - m1)
    g1 = 1.0 / (1.0 + e2)
    g2 = e2 / (1.0 + e2)
    ids_ref[...] = jnp.where(lane == 0, i1, jnp.where(lane == 1, i2, 0))
    gts_ref[...] = jnp.where(lane == 0, g1, jnp.where(lane == 1, g2, 0.0))


def _mix_router(h, y, wo, g, r_hi, r_lo, xn_all, row0):
    n, d = h.shape
    tm = min(512, n)
    row = lambda w: pl.BlockSpec((tm, w), lambda i: (i, 0))
    return pl.pallas_call(
        _mix_router_kernel,
        grid=(n // tm,),
        in_specs=[row(d), row(d),
                  pl.BlockSpec((d, d), lambda i: (0, 0)),
                  pl.BlockSpec((1, d), lambda i: (0, 0)),
                  pl.BlockSpec((d, LANES), lambda i: (0, 0)),
                  pl.BlockSpec((d, LANES), lambda i: (0, 0)),
                  pl.BlockSpec(memory_space=pl.ANY)],
        out_specs=[row(d), pl.BlockSpec((tm, d), lambda i: (i + row0 // tm, 0)), row(LANES), row(LANES)],
        out_shape=[
            jax.ShapeDtypeStruct((n, d), F32),
            jax.ShapeDtypeStruct(xn_all.shape, F32),
            jax.ShapeDtypeStruct((n, LANES), I32),
            jax.ShapeDtypeStruct((n, LANES), F32),
        ],
        input_output_aliases={6: 1},
        compiler_params=_cparams("parallel"),
        name="mix_router",
    )(h, y, wo, g, r_hi, r_lo, xn_all)


def _wait_rows(src, dst, sem, copies):
    for _ in range(copies):
        pltpu.make_async_copy(src, dst, sem).wait()


def _dispatch_kernel(pos_ref, x_ref, xs_in, xs_hbm, sem, *, tt):
    del xs_in
    i = pl.program_id(0)

    def body(r, c):
        t = i * tt + r
        for j in range(2):
            pltpu.make_async_copy(x_ref.at[pl.ds(r, 1), :], xs_hbm.at[pl.ds(pos_ref[2 * t + j], 1), :],
                                  sem.at[0]).start()
        return c

    lax.fori_loop(0, tt, body, 0, unroll=8)
    _wait_rows(x_ref, xs_hbm.at[pl.ds(0, tt), :], sem.at[0], 2)


def _dispatch(x_all, pos, xs_zero):
    n, d = x_all.shape
    tt = 384 if n % 384 == 0 else LANES
    kern = functools.partial(_dispatch_kernel, tt=tt)
    grid_spec = pltpu.PrefetchScalarGridSpec(
        num_scalar_prefetch=1,
        grid=(n // tt,),
        in_specs=[pl.BlockSpec((tt, d), lambda i, p: (i, 0)), pl.BlockSpec(memory_space=pl.ANY)],
        out_specs=pl.BlockSpec(memory_space=pl.ANY),
        scratch_shapes=[pltpu.SemaphoreType.DMA((1,))],
    )
    return pl.pallas_call(
        kern,
        grid_spec=grid_spec,
        out_shape=jax.ShapeDtypeStruct(xs_zero.shape, F32),
        input_output_aliases={2: 0},
        compiler_params=_cparams("arbitrary"),
        name="moe_dispatch",
    )(pos, x_all, xs_zero)


def _expert_kernel(te_ref, nu_ref, xs_ref, wg_ref, wu_ref, wd_ref, ys_ref, xb, acc_ref):
    i = pl.program_id(0)
    k = pl.program_id(1)
    used = i < nu_ref[0]

    @pl.when(k == 0)
    def _():
        acc_ref[...] = jnp.zeros(acc_ref.shape, F32)
        xb[...] = xs_ref[...].astype(BF16)

    @pl.when(used)
    def _():
        x = xb[...]
        a = jnp.dot(x, wg_ref[0, 0], preferred_element_type=F32)
        b = jnp.dot(x, wu_ref[0, 0], preferred_element_type=F32)
        acc_ref[...] += jnp.dot((_silu(a) * b).astype(BF16), wd_ref[0, 0], preferred_element_type=F32)

    @pl.when(k == pl.num_programs(1) - 1)
    def _():
        ys_ref[...] = acc_ref[...]


def _experts(xs, tile_expert, n_used, wg, wu, wd, layer):
    n_rows = xs.shape[0]
    tm = MOE_TM
    nk = 4
    tk = D_EXP // nk

    def kk(i, k, nu):
        return jnp.where(i < nu[0], k, nk - 1)

    grid_spec = pltpu.PrefetchScalarGridSpec(
        num_scalar_prefetch=2,
        grid=(n_rows // tm, nk),
        in_specs=[
            pl.BlockSpec((tm, D_MODEL), lambda i, k, te, nu: (i, 0)),
            pl.BlockSpec((1, 1, D_MODEL, tk), lambda i, k, te, nu: (layer, te[i], 0, kk(i, k, nu))),
            pl.BlockSpec((1, 1, D_MODEL, tk), lambda i, k, te, nu: (layer, te[i], 0, kk(i, k, nu))),
            pl.BlockSpec((1, 1, tk, D_MODEL), lambda i, k, te, nu: (layer, te[i], kk(i, k, nu), 0)),
        ],
        out_specs=pl.BlockSpec((tm, D_MODEL), lambda i, k, te, nu: (i, 0)),
        scratch_shapes=[pltpu.VMEM((tm, D_MODEL), BF16), pltpu.VMEM((tm, D_MODEL), F32)],
    )
    return pl.pallas_call(
        _expert_kernel,
        grid_spec=grid_spec,
        out_shape=jax.ShapeDtypeStruct((n_rows, D_MODEL), F32),
        compiler_params=_cparams("arbitrary", "arbitrary"),
        name="moe_experts",
    )(tile_expert, n_used, xs, wg, wu, wd)


def _combine_kernel(pos_ref, h_ref, g_ref, ys_hbm, o_ref, buf, sem, *, tt, tok0):
    i = pl.program_id(0)

    def issue(tile, slot):
        def body(r, c):
            t = tok0 + tile * tt + r
            for j in range(2):
                pltpu.make_async_copy(ys_hbm.at[pl.ds(pos_ref[2 * t + j], 1), :],
                                      buf.at[slot, j, pl.ds(r, 1), :], sem.at[slot]).start()
            return c

        lax.fori_loop(0, tt, body, 0, unroll=8)

    @pl.when(i == 0)
    def _():
        issue(0, 0)

    @pl.when(i + 1 < pl.num_programs(0))
    def _():
        issue(i + 1, (i + 1) % 2)

    slot = i % 2
    _wait_rows(ys_hbm.at[pl.ds(0, tt), :], buf.at[slot, 0], sem.at[slot], 2)
    g = g_ref[...]
    o_ref[...] = h_ref[...] + (g[:, 0:1] * buf[slot, 0] + g[:, 1:2] * buf[slot, 1])


def _combine(h1, gts, pos, ys, tok0):
    n, d = h1.shape
    tt = min(512, n)
    kern = functools.partial(_combine_kernel, tt=tt, tok0=tok0)
    grid_spec = pltpu.PrefetchScalarGridSpec(
        num_scalar_prefetch=1,
        grid=(n // tt,),
        in_specs=[pl.BlockSpec((tt, d), lambda i, p: (i, 0)),
                  pl.BlockSpec((tt, LANES), lambda i, p: (i, 0)),
                  pl.BlockSpec(memory_space=pl.ANY)],
        out_specs=pl.BlockSpec((tt, d), lambda i, p: (i, 0)),
        scratch_shapes=[pltpu.VMEM((2, 2, tt, d), F32), pltpu.SemaphoreType.DMA((2,))],
    )
    return pl.pallas_call(
        kern,
        grid_spec=grid_spec,
        out_shape=jax.ShapeDtypeStruct((n, d), F32),
        compiler_params=_cparams("arbitrary"),
        name="moe_combine",
    )(pos, h1, gts, ys)


def _moe(h1_p, h1_s, x_all, ids_p, ids_s, gts_p, gts_s, xs_zero, wg, wu, wd, layer):
    n_p = h1_p.shape[0]
    tm = MOE_TM
    n_tiles = xs_zero.shape[0] // tm
    e = jnp.concatenate([ids_p[:, :2], ids_s[:, :2]], axis=0)
    hit = jnp.sum((e[:, :, None] == jnp.arange(N_EXP, dtype=I32)).astype(I32), axis=1)
    cum = jnp.cumsum(hit, axis=0)
    rank = cum - hit
    counts = cum[-1]
    padded = ((counts + tm - 1) // tm) * tm
    ends = jnp.cumsum(padded)
    offs = ends - padded
    pos = (offs[e] + jnp.take_along_axis(rank, e, axis=1)).reshape(-1).astype(I32)
    n_used = (ends[-1] // tm).astype(I32).reshape(1)
    tile_start = jnp.arange(n_tiles, dtype=I32) * tm
    tile_expert = jnp.sum((ends[None, :] <= tile_start[:, None]).astype(I32), axis=1)
    tile_expert = jnp.minimum(tile_expert, N_EXP - 1).astype(I32)

    xs = _dispatch(x_all, pos, xs_zero)
    ys = _experts(xs, tile_expert, n_used, wg, wu, wd, layer)
    out_p = _combine(h1_p, gts_p, pos, ys, 0)
    out_s = _combine(h1_s, gts_s, pos, ys, n_p)
    return out_p, out_s


def _final_norm_kernel(x_ref, g_ref, o_ref):
    o_ref[...] = _rms(x_ref[...], g_ref[...])


def _final_norm(x, g):
    n, d = x.shape
    tm = min(1024, n)
    return pl.pallas_call(
        _final_norm_kernel,
        grid=(n // tm,),
        in_specs=[pl.BlockSpec((tm, d), lambda i: (i, 0)), pl.BlockSpec((1, d), lambda i: (0, 0))],
        out_specs=pl.BlockSpec((tm, d), lambda i: (i, 0)),
        out_shape=jax.ShapeDtypeStruct((n, d), F32),
        compiler_params=_cparams("parallel"),
        name="final_norm",
    )(x, g)


def kernel(x_prompt, x_sample, state_conv, state_ssm_re, state_ssm_im, state_hgrn, norm_mix, norm_ffn, norm_final, ab_w_in, ab_w_out, conv_w, conv_b, conv_ln_g, conv_ln_b, ssm_a_re, ssm_a_im, ssm_log_dt, ssm_b_re, ssm_b_im, ssm_c_re, ssm_c_im, ssm_d, ssm_glu_w, ssm_glu_b, hg_w_in, hg_lb_logits, hg_gnorm, hg_w_out, ffn_w_gate, ffn_w_up, ffn_w_down, moe_router, moe_w_gate, moe_w_up, moe_w_down):
    bsz, seq, d = x_prompt.shape
    n_s = x_sample.shape[0]
    depth = norm_mix.shape[0]
    hp = x_prompt.reshape(bsz * seq, d)
    hs = x_sample.reshape(n_s, d)

    lbs = jax.nn.softmax(hg_lb_logits.astype(F32), axis=0)
    lbs = jnp.cumsum(lbs, axis=0) - lbs[0:1]

    conv_p, re_p, im_p, hg_p = [], [], [], []
    conv_s, re_s, im_s, hg_s = [], [], [], []
    zero_state = jnp.zeros((bsz, S5_STATE), F32)
    x_all = jnp.zeros((bsz * seq + n_s, d), F32)
    moe_wg, moe_wu, moe_wd = moe_w_gate.astype(BF16), moe_w_up.astype(BF16), moe_w_down.astype(BF16)
    row = lambda v: v.reshape(1, -1).astype(F32)

    for l in range(depth):
        j = l // 2
        g_mix, g_ffn = row(norm_mix[l]), row(norm_ffn[l])
        if l % 2 == 0:
            w_in = ab_w_in[j].astype(BF16)
            woa = ab_w_out[j, :D_A].astype(BF16)
            wob = ab_w_out[j, D_A:].astype(BF16)
            cw, cb, clg, clb = conv_w[j].astype(F32), row(conv_b[j]), row(conv_ln_g[j]), row(conv_ln_b[j])
            abar_re, abar_im, bcat, ccat = _s5_params(
                ssm_a_re[j].astype(F32), ssm_a_im[j].astype(F32), ssm_log_dt[j].astype(F32),
                ssm_b_re[j].astype(F32), ssm_b_im[j].astype(F32), ssm_c_re[j].astype(F32), ssm_c_im[j].astype(F32))
            sd, sgw, sgb = row(ssm_d[j]), ssm_glu_w[j].astype(BF16), row(ssm_glu_b[j])
            pad = ((0, 0), (0, D_FF_PAD - D_FF))
            wg = jnp.pad(ffn_w_gate[j], pad).astype(BF16)
            wu = jnp.pad(ffn_w_up[j], pad).astype(BF16)
            wd = jnp.pad(ffn_w_down[j], (pad[1], pad[0])).astype(BF16)

            proj = _norm_matmul(hp, g_mix, w_in, tn=w_in.shape[1], out_dtype=BF16)
            proj3 = proj.reshape(bsz, seq, -1)
            ya, cb_new = _conv_prompt(proj3, cw, cb, clg, clb)
            yb, hr, hi = _s5(proj3, zero_state, zero_state, abar_re, abar_im, bcat, ccat, sd, sgw, sgb,
                             tt=min(128, seq), seq_major=True)
            hp = _mix_ffn(hp, ya.reshape(bsz * seq, D_A), yb.reshape(bsz * seq, D_B), woa, wob, g_ffn,
                          wg, wu, wd)
            conv_p.append(cb_new)
            re_p.append(hr.reshape(bsz, S5_G, S5_P))
            im_p.append(hi.reshape(bsz, S5_G, S5_P))

            proj = _norm_matmul(hs, g_mix, w_in, tn=w_in.shape[1])
            ya, nb_t = _conv_sample(proj, jnp.transpose(state_conv[j].astype(F32), (1, 0, 2)), cw, cb, clg, clb)
            yb_t, hr, hi = _s5(proj[:, 2 * D_A:].reshape(1, n_s, D_B),
                               state_ssm_re[j].astype(F32).reshape(n_s, S5_STATE),
                               state_ssm_im[j].astype(F32).reshape(n_s, S5_STATE),
                               abar_re, abar_im, bcat, ccat, sd, sgw, sgb, tt=1, seq_major=False)
            hs = _mix_ffn(hs, ya, yb_t.reshape(n_s, D_B), woa, wob, g_ffn, wg, wu, wd)
            conv_s.append(jnp.transpose(nb_t, (1, 0, 2)))
            re_s.append(hr.reshape(n_s, S5_G, S5_P))
            im_s.append(hi.reshape(n_s, S5_G, S5_P))
        else:
            w_in = hg_w_in[j].astype(BF16)
            wo = hg_w_out[j].astype(BF16)
            lb, gn = row(lbs[j]), row(hg_gnorm[j])
            router_pad = jnp.pad(moe_router[j].astype(F32), ((0, 0), (0, LANES - N_EXP)))
            r_hi = router_pad.astype(BF16)
            r_lo = (router_pad - r_hi.astype(F32)).astype(BF16)

            proj = _norm_matmul(hp, g_mix, w_in, tn=1024, out_dtype=BF16)
            o, s_new = _hgrn_prompt(proj.reshape(bsz, seq, -1), lb, gn)
            h1_p, x_all, ids_p, gts_p = _mix_router(hp, o.reshape(bsz * seq, d), wo, g_ffn, r_hi, r_lo,
                                                    x_all, 0)
            hg_p.append(s_new)

            proj = _norm_matmul(hs, g_mix, w_in, tn=1024)
            o, s_new = _hgrn_sample(proj, state_hgrn[j].astype(F32), lb, gn)
            h1_s, x_all, ids_s, gts_s = _mix_router(hs, o, wo, g_ffn, r_hi, r_lo, x_all, bsz * seq)
            hg_s.append(s_new)

            n_rows = (2 * x_all.shape[0] + N_EXP * (MOE_TM - 1)) // MOE_TM * MOE_TM
            hp, hs = _moe(h1_p, h1_s, x_all, ids_p, ids_s, gts_p, gts_s, jnp.zeros((n_rows, d), F32),
                          moe_wg, moe_wu, moe_wd, j)

    g_fin = row(norm_final)
    y_prompt = _final_norm(hp, g_fin).reshape(bsz, seq, d)
    y_sample = _final_norm(hs, g_fin).reshape(n_s, 1, d)
    return (y_prompt, y_sample, jnp.stack(conv_p), jnp.stack(re_p), jnp.stack(im_p), jnp.stack(hg_p),
            jnp.stack(conv_s), jnp.stack(re_s), jnp.stack(im_s), jnp.stack(hg_s))
```

```python
import functools
import math

import jax
import jax.numpy as jnp
from jax import lax
from jax.experimental import pallas as pl
from jax.experimental.pallas import tpu as pltpu

F32 = jnp.float32
BF16 = jnp.bfloat16
I32 = jnp.int32

EPS = 1e-6
LN_EPS = 1e-5
FORGET_FLOOR = 1e-30

D_MODEL = 1024
D_A = 512
D_B = 512
CONV_W = 31
CONV_HALO = 32
S5_G = 32
S5_P = 64
S5_GH = 16
S5_STATE = S5_G * S5_P
S5_CHUNKS = 4
HG_HEADS = 8
HG_K = 128
HG_V = 128
HG_CHUNK = 64
D_FF = 2752
D_FF_PAD = 2816
N_EXP = 8
D_EXP = 3584
LANES = 128
VMEM_LIMIT_BYTES = 56 * 1024 * 1024
EXP_RANGE = 80.0

MOE_TM = 1024


def _cparams(*sem):
    return pltpu.CompilerParams(dimension_semantics=sem, vmem_limit_bytes=VMEM_LIMIT_BYTES)


def _sigmoid(x):
    return 1.0 / (1.0 + jnp.exp(-x))


def _silu(x):
    return x * _sigmoid(x)


def _rms(x, g):
    return x * lax.rsqrt(jnp.mean(x * x, axis=-1, keepdims=True) + EPS) * g


def _bdot(a, b):
    return jnp.dot(a.astype(BF16), b.astype(BF16), preferred_element_type=F32)


def _norm_matmul_kernel(x_ref, g_ref, w_ref, o_ref, xn_ref):
    @pl.when(pl.program_id(1) == 0)
    def _():
        xn_ref[...] = _rms(x_ref[...], g_ref[...]).astype(BF16)

    o_ref[...] = jnp.dot(xn_ref[...], w_ref[...], preferred_element_type=F32).astype(o_ref.dtype)


def _norm_matmul(x, g, w, tn, out_dtype=F32):
    n, d = x.shape
    nout = w.shape[1]
    tm = min(1024, n)
    return pl.pallas_call(
        _norm_matmul_kernel,
        grid=(n // tm, nout // tn),
        in_specs=[
            pl.BlockSpec((tm, d), lambda i, j: (i, 0)),
            pl.BlockSpec((1, d), lambda i, j: (0, 0)),
            pl.BlockSpec((d, tn), lambda i, j: (0, j)),
        ],
        out_specs=pl.BlockSpec((tm, tn), lambda i, j: (i, j)),
        out_shape=jax.ShapeDtypeStruct((n, nout), out_dtype),
        scratch_shapes=[pltpu.VMEM((tm, d), BF16)],
        compiler_params=_cparams("parallel", "arbitrary"),
        name="norm_matmul",
    )(x, g, w)


def _conv_post(y, b, lg, lb):
    y = y + b
    mu = jnp.mean(y, axis=-1, keepdims=True)
    yc = y - mu
    var = jnp.mean(yc * yc, axis=-1, keepdims=True)
    return _silu(yc * lax.rsqrt(var + LN_EPS) * lg + lb)


def _conv_prompt_kernel(av_ref, ag_ref, w_ref, b_ref, lg_ref, lb_ref, y_ref, nb_ref, ubuf, *, tt, rb):
    t = pl.program_id(1)

    @pl.when(t == 0)
    def _():
        ubuf[0:CONV_HALO, :] = jnp.zeros((CONV_HALO, D_A), F32)

    @pl.when(t > 0)
    def _():
        ubuf[0:CONV_HALO, :] = ubuf[tt:tt + CONV_HALO, :]

    ubuf[CONV_HALO:CONV_HALO + tt, :] = av_ref[0].astype(F32) * _sigmoid(ag_ref[0].astype(F32))
    off = CONV_HALO - (CONV_W - 1)
    span = rb + CONV_HALO
    for r in range(tt // rb):
        win = ubuf[r * rb:r * rb + span, :]
        acc = jnp.zeros((rb, D_A), F32)
        for p in range(8):
            wp = win if p == 0 else pltpu.roll(win, span - p, 0)
            for q in range(CONV_HALO // 8 + 1):
                k = 8 * q + p - off
                if 0 <= k < CONV_W:
                    acc = acc + w_ref[k:k + 1, :] * wp[8 * q:8 * q + rb, :]
        y_ref[0, r * rb:(r + 1) * rb, :] = _conv_post(acc, b_ref[...], lg_ref[...], lb_ref[...]).astype(y_ref.dtype)

    @pl.when(t == pl.num_programs(1) - 1)
    def _():
        nb_ref[0] = ubuf[tt + off:tt + CONV_HALO, :]


def _conv_prompt(proj3, w, b, lg, lb):
    bsz, t, _ = proj3.shape
    tt = min(256, t)
    kern = functools.partial(_conv_prompt_kernel, tt=tt, rb=32)
    vec = pl.BlockSpec((1, D_A), lambda i, j: (0, 0))
    return pl.pallas_call(
        kern,
        grid=(bsz, t // tt),
        in_specs=[
            pl.BlockSpec((1, tt, D_A), lambda i, j: (i, j, 0)),
            pl.BlockSpec((1, tt, D_A), lambda i, j: (i, j, 1)),
            pl.BlockSpec((CONV_W, D_A), lambda i, j: (0, 0)),
            vec, vec, vec,
        ],
        out_specs=[
            pl.BlockSpec((1, tt, D_A), lambda i, j: (i, j, 0)),
            pl.BlockSpec((1, CONV_W - 1, D_A), lambda i, j: (i, 0, 0)),
        ],
        out_shape=[
            jax.ShapeDtypeStruct((bsz, t, D_A), BF16),
            jax.ShapeDtypeStruct((bsz, CONV_W - 1, D_A), F32),
        ],
        scratch_shapes=[pltpu.VMEM((CONV_HALO + tt, D_A), F32)],
        compiler_params=_cparams("parallel", "arbitrary"),
        name="conv_prompt",
    )(proj3, proj3, w, b, lg, lb)


def _conv_sample_kernel(av_ref, ag_ref, buf_ref, w_ref, b_ref, lg_ref, lb_ref, y_ref, nb_ref):
    u = av_ref[...] * _sigmoid(ag_ref[...])
    acc = w_ref[CONV_W - 1:CONV_W, :] * u
    for k in range(CONV_W - 1):
        acc = acc + w_ref[k:k + 1, :] * buf_ref[k]
    y_ref[...] = _conv_post(acc, b_ref[...], lg_ref[...], lb_ref[...]).astype(y_ref.dtype)
    for k in range(CONV_W - 2):
        nb_ref[k] = buf_ref[k + 1]
    nb_ref[CONV_W - 2] = u


def _conv_sample(proj, buf_t, w, b, lg, lb):
    n = proj.shape[0]
    nbk = min(32, n)
    vec = pl.BlockSpec((1, D_A), lambda i: (0, 0))
    return pl.pallas_call(
        _conv_sample_kernel,
        grid=(n // nbk,),
        in_specs=[
            pl.BlockSpec((nbk, D_A), lambda i: (i, 0)),
            pl.BlockSpec((nbk, D_A), lambda i: (i, 1)),
            pl.BlockSpec((CONV_W - 1, nbk, D_A), lambda i: (0, i, 0)),
            pl.BlockSpec((CONV_W, D_A), lambda i: (0, 0)),
            vec, vec, vec,
        ],
        out_specs=[
            pl.BlockSpec((nbk, D_A), lambda i: (i, 0)),
            pl.BlockSpec((CONV_W - 1, nbk, D_A), lambda i: (0, i, 0)),
        ],
        out_shape=[
            jax.ShapeDtypeStruct((n, D_A), BF16),
            jax.ShapeDtypeStruct((CONV_W - 1, n, D_A), F32),
        ],
        compiler_params=_cparams("parallel"),
        name="conv_sample",
    )(proj, proj, buf_t, w, b, lg, lb)


def _s5_kernel(u_ref, h0r_ref, h0i_ref, ar_ref, ai_ref, bcat_ref, ccat_ref, d_ref, gw_ref, gb_ref,
               y_ref, hr_out, hi_out, sre, sim, cre, cim, rt, *, tt, nb, lw, seq_major):
    i = pl.program_id(0)
    m = tt * nb
    cw = S5_STATE // S5_CHUNKS

    @pl.when(i == 0)
    def _():
        cre[...] = h0r_ref[...]
        cim[...] = h0i_ref[...]

    if seq_major:
        for b in range(nb):
            rt[:, b, :] = u_ref[b].astype(F32)
        u = rt[...].reshape(m, D_B)
    else:
        u = u_ref[...].reshape(m, D_B)
    ub = u.astype(BF16)
    for c in range(S5_CHUNKS):
        bu = jnp.dot(ub[:, c * LANES:(c + 1) * LANES], bcat_ref[c], preferred_element_type=F32)
        sre[:, :, c * cw:(c + 1) * cw] = bu[:, :cw].reshape(tt, nb, cw)
        sim[:, :, c * cw:(c + 1) * cw] = bu[:, cw:].reshape(tt, nb, cw)

    for c in range(S5_STATE // lw):
        ls = slice(c * lw, (c + 1) * lw)
        ar = jnp.broadcast_to(ar_ref[:, ls], (nb, lw))
        ai = jnp.broadcast_to(ai_ref[:, ls], (nb, lw))

        def body(t, carry, ls=ls, ar=ar, ai=ai):
            hr, hi = carry
            nr = ar * hr - ai * hi + sre[t, :, ls]
            ni = ar * hi + ai * hr + sim[t, :, ls]
            sre[t, :, ls] = nr
            sim[t, :, ls] = ni
            return nr, ni

        hr, hi = lax.fori_loop(0, tt, body, (cre[:, ls], cim[:, ls]), unroll=min(tt, 8))
        cre[:, ls] = hr
        cim[:, ls] = hi

    hre = sre[...].reshape(m, S5_STATE)
    him = sim[...].reshape(m, S5_STATE)
    ys = []
    for c in range(S5_CHUNKS):
        hcat = jnp.concatenate([hre[:, c * cw:(c + 1) * cw], him[:, c * cw:(c + 1) * cw]], axis=1)
        ys.append(jnp.dot(hcat.astype(BF16), ccat_ref[c], preferred_element_type=F32))
    y = jnp.concatenate(ys, axis=1) + d_ref[...] * u
    y = 0.5 * y * (1.0 + lax.erf(y * (1.0 / math.sqrt(2.0))))
    z = jnp.dot(y.astype(BF16), gw_ref[...], preferred_element_type=F32) + gb_ref[...]
    out = (y * _sigmoid(z)).reshape(tt, nb, D_B)
    if seq_major:
        rt[...] = out
        for b in range(nb):
            y_ref[b] = rt[:, b, :].astype(y_ref.dtype)
    else:
        y_ref[...] = out.astype(y_ref.dtype)

    @pl.when(i == pl.num_programs(0) - 1)
    def _():
        hr_out[...] = cre[...]
        hi_out[...] = cim[...]


def _s5(u, h0r, h0i, abar_re, abar_im, bcat, ccat, d, glu_w, glu_b, tt, seq_major):
    if seq_major:
        nb, t, c = u.shape
        u_spec = pl.BlockSpec((nb, tt, D_B), lambda i: (0, i, c // D_B - 1))
        y_spec = pl.BlockSpec((nb, tt, D_B), lambda i: (0, i, 0))
        y_shape = (nb, t, D_B)
    else:
        t, nb, _ = u.shape
        u_spec = pl.BlockSpec((tt, nb, D_B), lambda i: (i, 0, 0))
        y_spec = u_spec
        y_shape = (t, nb, D_B)
    lw = max(LANES, min(512, 8 * 1024 // nb))
    kern = functools.partial(_s5_kernel, tt=tt, nb=nb, lw=lw, seq_major=seq_major)
    full = lambda shape: pl.BlockSpec(shape, lambda i: (0,) * len(shape))
    return pl.pallas_call(
        kern,
        grid=(t // tt,),
        in_specs=[
            u_spec,
            full((nb, S5_STATE)), full((nb, S5_STATE)),
            full((1, S5_STATE)), full((1, S5_STATE)),
            full(bcat.shape), full(ccat.shape),
            full((1, D_B)), full((D_B, D_B)), full((1, D_B)),
        ],
        out_specs=[
            y_spec,
            full((nb, S5_STATE)), full((nb, S5_STATE)),
        ],
        out_shape=[
            jax.ShapeDtypeStruct(y_shape, BF16),
            jax.ShapeDtypeStruct((nb, S5_STATE), F32),
            jax.ShapeDtypeStruct((nb, S5_STATE), F32),
        ],
        scratch_shapes=[
            pltpu.VMEM((tt, nb, S5_STATE), F32), pltpu.VMEM((tt, nb, S5_STATE), F32),
            pltpu.VMEM((nb, S5_STATE), F32), pltpu.VMEM((nb, S5_STATE), F32),
            pltpu.VMEM((tt, nb, D_B), F32),
        ],
        compiler_params=_cparams("arbitrary"),
        name="s5",
    )(u, h0r, h0i, abar_re, abar_im, bcat, ccat, d, glu_w, glu_b)


def _s5_params(a_re, a_im, log_dt, b_re, b_im, c_re, c_im):
    dt = jnp.exp(log_dt)[:, None]
    mag = jnp.exp(dt * a_re)
    ang = dt * a_im
    abar_re, abar_im = mag * jnp.cos(ang), mag * jnp.sin(ang)
    den = a_re * a_re + a_im * a_im
    nr, ni = abar_re - 1.0, abar_im
    coef_re = (nr * a_re + ni * a_im) / den
    coef_im = (ni * a_re - nr * a_im) / den
    bbar_re = coef_re[..., None] * b_re - coef_im[..., None] * b_im
    bbar_im = coef_re[..., None] * b_im + coef_im[..., None] * b_re
    gpc = S5_G // S5_CHUNKS
    eye = jnp.eye(gpc, dtype=F32)

    def bblk(x):
        x = x.reshape(S5_CHUNKS, gpc, S5_P, S5_GH)
        return jnp.einsum("cgph,gk->cghkp", x, eye).reshape(S5_CHUNKS, gpc * S5_GH, gpc * S5_P)

    def cblk(x):
        x = x.reshape(S5_CHUNKS, gpc, S5_GH, S5_P)
        return jnp.einsum("cghp,gk->cgpkh", x, eye).reshape(S5_CHUNKS, gpc * S5_P, gpc * S5_GH)

    bcat = jnp.concatenate([bblk(bbar_re), bblk(bbar_im)], axis=2).astype(BF16)
    ccat = jnp.concatenate([cblk(c_re), -cblk(c_im)], axis=1).astype(BF16)
    return abar_re.reshape(1, S5_STATE), abar_im.reshape(1, S5_STATE), bcat, ccat


def _hg_gates(q, f, lb):
    qf = _silu(q)
    sig = _sigmoid(f)
    forget = lb + (1.0 - lb) * sig
    logg = jnp.log(jnp.maximum(forget, FORGET_FLOOR))
    kf = (1.0 - lb) * (1.0 - sig)
    return qf, logg, kf


def _hg_out(o, g, gn):
    return o * lax.rsqrt(jnp.mean(o * o, axis=-1, keepdims=True) + EPS) * gn * _silu(g)


def _cumsum_rows(x, tri):
    hi = x.astype(BF16)
    r1 = x - hi.astype(F32)
    mid = r1.astype(BF16)
    lo = (r1 - mid.astype(F32)).astype(BF16)
    dot = lambda p: jnp.dot(tri, p, preferred_element_type=F32)
    return dot(hi) + dot(mid) + dot(lo)


def _hgrn_prompt_kernel(q_ref, f_ref, v_ref, g_ref, lb_ref, gn_ref, o_ref, s_out, st, obuf, *, tt):
    t = pl.program_id(1)
    L = HG_CHUNK

    @pl.when(t == 0)
    def _():
        st[...] = jnp.zeros(st.shape, F32)

    row = lax.broadcasted_iota(I32, (L, L), 0)
    col = lax.broadcasted_iota(I32, (L, L), 1)
    causal = row >= col
    tri = causal.astype(BF16)
    nt = (((1,), (1,)), ((), ()))
    tn = (((0,), (0,)), ((), ()))
    for c in range(tt // L):
        rs = slice(c * L, (c + 1) * L)
        q, logg, k = _hg_gates(q_ref[0, rs, :].astype(F32), f_ref[0, rs, :].astype(F32), lb_ref[...])
        vb = v_ref[0, rs, :].astype(BF16)
        gc = _cumsum_rows(logg, tri)
        gmid = gc[L // 2 - 1:L // 2, :]
        glast = gc[L - 1:L, :]
        spread = jnp.max(jnp.maximum(gc[0:1, :] - gmid, gmid - glast))
        fast = spread <= EXP_RANGE

        @pl.when(fast)
        def _():
            qe = (q * jnp.exp(gc - gmid)).astype(BF16)
            ke = (k * jnp.exp(gmid - gc)).astype(BF16)
            qg = (q * jnp.exp(gc)).astype(BF16)
            kd = (k * jnp.exp(glast - gc)).astype(BF16)
            dlast = jnp.exp(glast)
            for h in range(HG_HEADS):
                hs = slice(h * HG_K, (h + 1) * HG_K)
                sc = lax.dot_general(qe[:, hs], ke[:, hs], nt, preferred_element_type=F32)
                sc = jnp.where(causal, sc, 0.0).astype(BF16)
                s_t = st[h]
                o = (jnp.dot(sc, vb[:, hs], preferred_element_type=F32)
                     + lax.dot_general(qg[:, hs], s_t.astype(BF16), nt, preferred_element_type=F32))
                st[h] = dlast[:, hs] * s_t + lax.dot_general(vb[:, hs], kd[:, hs], tn,
                                                             preferred_element_type=F32)
                o_ref[0, rs, hs] = _hg_out(o, g_ref[0, rs, hs].astype(F32), gn_ref[...]).astype(o_ref.dtype)

        @pl.when(jnp.logical_not(fast))
        def _():
            dec = jnp.exp(logg)
            kb = k.astype(BF16)
            rsel = lax.broadcasted_iota(I32, (L, 1), 0)
            obuf[...] = jnp.zeros(obuf.shape, F32)

            def step(t, carry):
                sel = rsel == t
                g_t = jnp.sum(jnp.where(sel, dec, 0.0), axis=0, keepdims=True)
                v_t = jnp.where(sel, vb, jnp.zeros_like(vb))
                q_t = jnp.where(sel, q, 0.0).astype(BF16)
                for h in range(HG_HEADS):
                    hs = slice(h * HG_K, (h + 1) * HG_K)
                    s_new = g_t[:, hs] * st[h] + lax.dot_general(v_t[:, hs], kb[:, hs], tn,
                                                                 preferred_element_type=F32)
                    st[h] = s_new
                    obuf[:, hs] += lax.dot_general(q_t[:, hs], s_new.astype(BF16), nt,
                                                   preferred_element_type=F32)
                return carry

            lax.fori_loop(0, L, step, 0)
            for h in range(HG_HEADS):
                hs = slice(h * HG_K, (h + 1) * HG_K)
                o_ref[0, rs, hs] = _hg_out(obuf[:, hs], g_ref[0, rs, hs].astype(F32),
                                           gn_ref[...]).astype(o_ref.dtype)

    @pl.when(t == pl.num_programs(1) - 1)
    def _():
        for h in range(HG_HEADS):
            s_out[0, h] = st[h].T


def _hgrn_prompt(proj3, lb, gn):
    bsz, t, _ = proj3.shape
    tt = min(256, t)
    dc = HG_HEADS * HG_K
    kern = functools.partial(_hgrn_prompt_kernel, tt=tt)
    blk = lambda c: pl.BlockSpec((1, tt, dc), lambda i, j, c=c: (i, j, c))
    return pl.pallas_call(
        kern,
        grid=(bsz, t // tt),
        in_specs=[blk(0), blk(1), blk(2), blk(3),
                  pl.BlockSpec((1, dc), lambda i, j: (0, 0)),
                  pl.BlockSpec((1, HG_V), lambda i, j: (0, 0))],
        out_specs=[
            pl.BlockSpec((1, tt, dc), lambda i, j: (i, j, 0)),
            pl.BlockSpec((1, HG_HEADS, HG_K, HG_V), lambda i, j: (i, 0, 0, 0)),
        ],
        out_shape=[
            jax.ShapeDtypeStruct((bsz, t, dc), BF16),
            jax.ShapeDtypeStruct((bsz, HG_HEADS, HG_K, HG_V), F32),
        ],
        scratch_shapes=[pltpu.VMEM((HG_HEADS, HG_V, HG_K), F32), pltpu.VMEM((HG_CHUNK, dc), F32)],
        compiler_params=_cparams("parallel", "arbitrary"),
        name="hgrn_prompt",
    )(proj3, proj3, proj3, proj3, lb, gn)


def _hgrn_sample_kernel(q_ref, f_ref, v_ref, g_ref, s_ref, lb_ref, gn_ref, o_ref, s_out, *, nbk):
    for h in range(HG_HEADS):
        hs = slice(h * HG_K, (h + 1) * HG_K)
        q, logg, k = _hg_gates(q_ref[:, hs], f_ref[:, hs], lb_ref[:, hs])
        dec = jnp.exp(logg)
        v = v_ref[:, hs]
        rows = []
        for n in range(nbk):
            col = lambda x: jnp.broadcast_to(x[n:n + 1, :], (HG_K, HG_K)).T
            s_new = col(dec) * s_ref[n, h] + col(k) * v[n:n + 1, :]
            s_out[n, h] = s_new
            rows.append(jnp.sum(col(q) * s_new, axis=0, keepdims=True))
        o = jnp.concatenate(rows, axis=0)
        o_ref[:, hs] = _hg_out(o, g_ref[:, hs], gn_ref[...]).astype(o_ref.dtype)


def _hgrn_sample(proj, s0, lb, gn):
    n = proj.shape[0]
    nbk = 8
    dc = HG_HEADS * HG_K
    kern = functools.partial(_hgrn_sample_kernel, nbk=nbk)
    blk = lambda c: pl.BlockSpec((nbk, dc), lambda i, c=c: (i, c))
    sblk = pl.BlockSpec((nbk, HG_HEADS, HG_K, HG_V), lambda i: (i, 0, 0, 0))
    return pl.pallas_call(
        kern,
        grid=(n // nbk,),
        in_specs=[blk(0), blk(1), blk(2), blk(3), sblk,
                  pl.BlockSpec((1, dc), lambda i: (0, 0)),
                  pl.BlockSpec((1, HG_V), lambda i: (0, 0))],
        out_specs=[pl.BlockSpec((nbk, dc), lambda i: (i, 0)), sblk],
        out_shape=[
            jax.ShapeDtypeStruct((n, dc), BF16),
            jax.ShapeDtypeStruct(s0.shape, F32),
        ],
        compiler_params=_cparams("parallel"),
        name="hgrn_sample",
    )(proj, proj, proj, proj, s0, lb, gn)


def _mix_ffn_kernel(h_ref, ya_ref, yb_ref, woa_ref, wob_ref, g_ref, wg_ref, wu_ref, wd_ref, o_ref,
                    h1_ref, xn_ref, acc_ref):
    k = pl.program_id(1)

    @pl.when(k == 0)
    def _():
        h1 = (h_ref[...] + jnp.dot(ya_ref[...], woa_ref[...], preferred_element_type=F32)
              + jnp.dot(yb_ref[...], wob_ref[...], preferred_element_type=F32))
        h1_ref[...] = h1
        xn_ref[...] = _rms(h1, g_ref[...]).astype(BF16)
        acc_ref[...] = jnp.zeros(acc_ref.shape, F32)

    xn = xn_ref[...]
    a = jnp.dot(xn, wg_ref[...], preferred_element_type=F32)
    b = jnp.dot(xn, wu_ref[...], preferred_element_type=F32)
    acc_ref[...] += jnp.dot((_silu(a) * b).astype(BF16), wd_ref[...], preferred_element_type=F32)

    @pl.when(k == pl.num_programs(1) - 1)
    def _():
        o_ref[...] = h1_ref[...] + acc_ref[...]


def _mix_ffn(h, ya, yb, woa, wob, g, wg, wu, wd):
    n, d = h.shape
    tm = min(512, n)
    tf = D_FF_PAD // 2
    row = lambda w: pl.BlockSpec((tm, w), lambda i, k: (i, 0))
    return pl.pallas_call(
        _mix_ffn_kernel,
        grid=(n // tm, D_FF_PAD // tf),
        in_specs=[
            row(d), row(D_A), row(D_B),
            pl.BlockSpec((D_A, d), lambda i, k: (0, 0)),
            pl.BlockSpec((D_B, d), lambda i, k: (0, 0)),
            pl.BlockSpec((1, d), lambda i, k: (0, 0)),
            pl.BlockSpec((d, tf), lambda i, k: (0, k)),
            pl.BlockSpec((d, tf), lambda i, k: (0, k)),
            pl.BlockSpec((tf, d), lambda i, k: (k, 0)),
        ],
        out_specs=row(d),
        out_shape=jax.ShapeDtypeStruct((n, d), F32),
        scratch_shapes=[pltpu.VMEM((tm, d), F32), pltpu.VMEM((tm, d), BF16), pltpu.VMEM((tm, d), F32)],
        compiler_params=_cparams("parallel", "arbitrary"),
        name="mix_ffn",
    )(h, ya, yb, woa, wob, g, wg, wu, wd)


def _mix_router_kernel(h_ref, y_ref, wo_ref, g_ref, rh_ref, rl_ref, *rest):
    h1_ref, xn_ref, ids_ref, gts_ref = rest[-4:]
    h1 = h_ref[...] + jnp.dot(y_ref[...], wo_ref[...], preferred_element_type=F32)
    h1_ref[...] = h1
    xn = _rms(h1, g_ref[...])
    xn_ref[...] = xn
    xh = xn.astype(BF16)
    xl = (xn - xh.astype(F32)).astype(BF16)
    dot = lambda a, b: jnp.dot(a, b, preferred_element_type=F32)
    logits = dot(xh, rh_ref[...]) + (dot(xh, rl_ref[...]) + dot(xl, rh_ref[...]))
    lane = lax.broadcasted_iota(I32, logits.shape, 1)
    neg = jnp.float32(-jnp.inf)
    logits = jnp.where(lane < N_EXP, logits, neg)
    m1 = jnp.max(logits, axis=-1, keepdims=True)
    i1 = jnp.min(jnp.where(logits == m1, lane, LANES), axis=-1, keepdims=True)
    rest = jnp.where(lane == i1, neg, logits)
    m2 = jnp.max(rest, axis=-1, keepdims=True)
    i2 = jnp.min(jnp.where(rest == m2, lane, LANES), axis=-1, keepdims=True)
    e2 = jnp.exp(m2 - m1)
    g1 = 1.0 / (1.0 + e2)
    g2 = e2 / (1.0 + e2)
    ids_ref[...] = jnp.where(lane == 0, i1, jnp.where(lane == 1, i2, 0))
    gts_ref[...] = jnp.where(lane == 0, g1, jnp.where(lane == 1, g2, 0.0))


def _mix_router(h, y, wo, g, r_hi, r_lo, xn_all, row0):
    n, d = h.shape
    tm = min(1024, n)
    row = lambda w: pl.BlockSpec((tm, w), lambda i: (i, 0))
    return pl.pallas_call(
        _mix_router_kernel,
        grid=(n // tm,),
        in_specs=[row(d), row(d),
                  pl.BlockSpec((d, d), lambda i: (0, 0)),
                  pl.BlockSpec((1, d), lambda i: (0, 0)),
                  pl.BlockSpec((d, LANES), lambda i: (0, 0)),
                  pl.BlockSpec((d, LANES), lambda i: (0, 0)),
                  pl.BlockSpec(memory_space=pl.ANY)],
        out_specs=[row(d), pl.BlockSpec((tm, d), lambda i: (i + row0 // tm, 0)), row(LANES), row(LANES)],
        out_shape=[
            jax.ShapeDtypeStruct((n, d), F32),
            jax.ShapeDtypeStruct(xn_all.shape, F32),
            jax.ShapeDtypeStruct((n, LANES), I32),
            jax.ShapeDtypeStruct((n, LANES), F32),
        ],
        input_output_aliases={6: 1},
        compiler_params=_cparams("parallel"),
        name="mix_router",
    )(h, y, wo, g, r_hi, r_lo, xn_all)


def _wait_rows(src, dst, sem, copies):
    for _ in range(copies):
        pltpu.make_async_copy(src, dst, sem).wait()


def _dispatch_kernel(pos_ref, x_ref, xs_in, xs_hbm, sem, *, tt):
    del xs_in
    i = pl.program_id(0)

    def body(r, c):
        t = i * tt + r
        for j in range(2):
            pltpu.make_async_copy(x_ref.at[pl.ds(r, 1), :], xs_hbm.at[pl.ds(pos_ref[2 * t + j], 1), :],
                                  sem.at[0]).start()
        return c

    lax.fori_loop(0, tt, body, 0, unroll=8)
    _wait_rows(x_ref, xs_hbm.at[pl.ds(0, tt), :], sem.at[0], 2)


def _dispatch(x_all, pos, xs_zero):
    n, d = x_all.shape
    tt = 384 if n % 384 == 0 else LANES
    kern = functools.partial(_dispatch_kernel, tt=tt)
    grid_spec = pltpu.PrefetchScalarGridSpec(
        num_scalar_prefetch=1,
        grid=(n // tt,),
        in_specs=[pl.BlockSpec((tt, d), lambda i, p: (i, 0)), pl.BlockSpec(memory_space=pl.ANY)],
        out_specs=pl.BlockSpec(memory_space=pl.ANY),
        scratch_shapes=[pltpu.SemaphoreType.DMA((1,))],
    )
    return pl.pallas_call(
        kern,
        grid_spec=grid_spec,
        out_shape=jax.ShapeDtypeStruct(xs_zero.shape, F32),
        input_output_aliases={2: 0},
        compiler_params=_cparams("arbitrary"),
        name="moe_dispatch",
    )(pos, x_all, xs_zero)


def _expert_kernel(te_ref, nu_ref, xs_ref, wg_ref, wu_ref, wd_ref, ys_ref, xb, acc_ref):
    i = pl.program_id(0)
    k = pl.program_id(1)
    used = i < nu_ref[0]

    @pl.when(k == 0)
    def _():
        acc_ref[...] = jnp.zeros(acc_ref.shape, F32)
        xb[...] = xs_ref[...].astype(BF16)

    @pl.when(used)
    def _():
        x = xb[...]
        a = jnp.dot(x, wg_ref[0, 0].astype(BF16), preferred_element_type=F32)
        b = jnp.dot(x, wu_ref[0, 0].astype(BF16), preferred_element_type=F32)
        acc_ref[...] += jnp.dot((_silu(a) * b).astype(BF16), wd_ref[0, 0].astype(BF16),
                                preferred_element_type=F32)

    @pl.when(k == pl.num_programs(1) - 1)
    def _():
        ys_ref[...] = acc_ref[...]


def _experts(xs, tile_expert, n_used, wg, wu, wd, layer):
    n_rows = xs.shape[0]
    tm = MOE_TM
    nk = 7
    tk = D_EXP // nk

    def kk(i, k, nu):
        return jnp.where(i < nu[0], k, nk - 1)

    grid_spec = pltpu.PrefetchScalarGridSpec(
        num_scalar_prefetch=2,
        grid=(n_rows // tm, nk),
        in_specs=[
            pl.BlockSpec((tm, D_MODEL), lambda i, k, te, nu: (i, 0)),
            pl.BlockSpec((1, 1, D_MODEL, tk), lambda i, k, te, nu: (layer, te[i], 0, kk(i, k, nu))),
            pl.BlockSpec((1, 1, D_MODEL, tk), lambda i, k, te, nu: (layer, te[i], 0, kk(i, k, nu))),
            pl.BlockSpec((1, 1, tk, D_MODEL), lambda i, k, te, nu: (layer, te[i], kk(i, k, nu), 0)),
        ],
        out_specs=pl.BlockSpec((tm, D_MODEL), lambda i, k, te, nu: (i, 0)),
        scratch_shapes=[pltpu.VMEM((tm, D_MODEL), BF16), pltpu.VMEM((tm, D_MODEL), F32)],
    )
    return pl.pallas_call(
        _expert_kernel,
        grid_spec=grid_spec,
        out_shape=jax.ShapeDtypeStruct((n_rows, D_MODEL), F32),
        compiler_params=_cparams("arbitrary", "arbitrary"),
        name="moe_experts",
    )(tile_expert, n_used, xs, wg, wu, wd)


def _combine_kernel(pos_ref, h_ref, g_ref, ys_hbm, o_ref, buf, sem, *, tt, tok0):
    i = pl.program_id(0)

    def issue(tile, slot):
        def body(r, c):
            t = tok0 + tile * tt + r
            for j in range(2):
                pltpu.make_async_copy(ys_hbm.at[pl.ds(pos_ref[2 * t + j], 1), :],
                                      buf.at[slot, j, pl.ds(r, 1), :], sem.at[slot]).start()
            return c

        lax.fori_loop(0, tt, body, 0, unroll=8)

    @pl.when(i == 0)
    def _():
        issue(0, 0)

    @pl.when(i + 1 < pl.num_programs(0))
    def _():
        issue(i + 1, (i + 1) % 2)

    slot = i % 2
    _wait_rows(ys_hbm.at[pl.ds(0, tt), :], buf.at[slot, 0], sem.at[slot], 2)
    g = g_ref[...]
    o_ref[...] = h_ref[...] + (g[:, 0:1] * buf[slot, 0] + g[:, 1:2] * buf[slot, 1])


def _combine(h1, gts, pos, ys, tok0):
    n, d = h1.shape
    tt = min(512, n)
    kern = functools.partial(_combine_kernel, tt=tt, tok0=tok0)
    grid_spec = pltpu.PrefetchScalarGridSpec(
        num_scalar_prefetch=1,
        grid=(n // tt,),
        in_specs=[pl.BlockSpec((tt, d), lambda i, p: (i, 0)),
                  pl.BlockSpec((tt, LANES), lambda i, p: (i, 0)),
                  pl.BlockSpec(memory_space=pl.ANY)],
        out_specs=pl.BlockSpec((tt, d), lambda i, p: (i, 0)),
        scratch_shapes=[pltpu.VMEM((2, 2, tt, d), F32), pltpu.SemaphoreType.DMA((2,))],
    )
    return pl.pallas_call(
        kern,
        grid_spec=grid_spec,
        out_shape=jax.ShapeDtypeStruct((n, d), F32),
        compiler_params=_cparams("arbitrary"),
        name="moe_combine",
    )(pos, h1, gts, ys)


def _moe(h1_p, h1_s, x_all, ids_p, ids_s, gts_p, gts_s, xs_zero, wg, wu, wd, layer):
    n_p = h1_p.shape[0]
    tm = MOE_TM
    n_tiles = xs_zero.shape[0] // tm
    e = jnp.concatenate([ids_p[:, :2], ids_s[:, :2]], axis=0)
    hit = jnp.sum((e[:, :, None] == jnp.arange(N_EXP, dtype=I32)).astype(I32), axis=1)
    cum = jnp.cumsum(hit, axis=0)
    rank = cum - hit
    counts = cum[-1]
    padded = ((counts + tm - 1) // tm) * tm
    ends = jnp.cumsum(padded)
    offs = ends - padded
    pos = (offs[e] + jnp.take_along_axis(rank, e, axis=1)).reshape(-1).astype(I32)
    n_used = (ends[-1] // tm).astype(I32).reshape(1)
    tile_start = jnp.arange(n_tiles, dtype=I32) * tm
    tile_expert = jnp.sum((ends[None, :] <= tile_start[:, None]).astype(I32), axis=1)
    tile_expert = jnp.minimum(tile_expert, N_EXP - 1).astype(I32)

    xs = _dispatch(x_all, pos, xs_zero)
    ys = _experts(xs, tile_expert, n_used, wg, wu, wd, layer)
    out_p = _combine(h1_p, gts_p, pos, ys, 0)
    out_s = _combine(h1_s, gts_s, pos, ys, n_p)
    return out_p, out_s


def _final_norm_kernel(x_ref, g_ref, o_ref):
    o_ref[...] = _rms(x_ref[...], g_ref[...])


def _final_norm(x, g):
    n, d = x.shape
    tm = min(1024, n)
    return pl.pallas_call(
        _final_norm_kernel,
        grid=(n // tm,),
        in_specs=[pl.BlockSpec((tm, d), lambda i: (i, 0)), pl.BlockSpec((1, d), lambda i: (0, 0))],
        out_specs=pl.BlockSpec((tm, d), lambda i: (i, 0)),
        out_shape=jax.ShapeDtypeStruct((n, d), F32),
        compiler_params=_cparams("parallel"),
        name="final_norm",
    )(x, g)


def kernel(x_prompt, x_sample, state_conv, state_ssm_re, state_ssm_im, state_hgrn, norm_mix, norm_ffn, norm_final, ab_w_in, ab_w_out, conv_w, conv_b, conv_ln_g, conv_ln_b, ssm_a_re, ssm_a_im, ssm_log_dt, ssm_b_re, ssm_b_im, ssm_c_re, ssm_c_im, ssm_d, ssm_glu_w, ssm_glu_b, hg_w_in, hg_lb_logits, hg_gnorm, hg_w_out, ffn_w_gate, ffn_w_up, ffn_w_down, moe_router, moe_w_gate, moe_w_up, moe_w_down):
    bsz, seq, d = x_prompt.shape
    n_s = x_sample.shape[0]
    depth = norm_mix.shape[0]
    hp = x_prompt.reshape(bsz * seq, d)
    hs = x_sample.reshape(n_s, d)

    lbs = jax.nn.softmax(hg_lb_logits.astype(F32), axis=0)
    lbs = jnp.cumsum(lbs, axis=0) - lbs[0:1]

    conv_p, re_p, im_p, hg_p = [], [], [], []
    conv_s, re_s, im_s, hg_s = [], [], [], []
    zero_state = jnp.zeros((bsz, S5_STATE), F32)
    x_all = jnp.zeros((bsz * seq + n_s, d), F32)
    moe_wg, moe_wu, moe_wd = moe_w_gate.astype(F32), moe_w_up.astype(F32), moe_w_down.astype(F32)
    row = lambda v: v.reshape(1, -1).astype(F32)

    for l in range(depth):
        j = l // 2
        g_mix, g_ffn = row(norm_mix[l]), row(norm_ffn[l])
        if l % 2 == 0:
            w_in = ab_w_in[j].astype(BF16)
            woa = ab_w_out[j, :D_A].astype(BF16)
            wob = ab_w_out[j, D_A:].astype(BF16)
            cw, cb, clg, clb = conv_w[j].astype(F32), row(conv_b[j]), row(conv_ln_g[j]), row(conv_ln_b[j])
            abar_re, abar_im, bcat, ccat = _s5_params(
                ssm_a_re[j].astype(F32), ssm_a_im[j].astype(F32), ssm_log_dt[j].astype(F32),
                ssm_b_re[j].astype(F32), ssm_b_im[j].astype(F32), ssm_c_re[j].astype(F32), ssm_c_im[j].astype(F32))
            sd, sgw, sgb = row(ssm_d[j]), ssm_glu_w[j].astype(BF16), row(ssm_glu_b[j])
            pad = ((0, 0), (0, D_FF_PAD - D_FF))
            wg = jnp.pad(ffn_w_gate[j], pad).astype(BF16)
            wu = jnp.pad(ffn_w_up[j], pad).astype(BF16)
            wd = jnp.pad(ffn_w_down[j], (pad[1], pad[0])).astype(BF16)

            proj = _norm_matmul(hp, g_mix, w_in, tn=w_in.shape[1], out_dtype=BF16)
            proj3 = proj.reshape(bsz, seq, -1)
            ya, cb_new = _conv_prompt(proj3, cw, cb, clg, clb)
            yb, hr, hi = _s5(proj3, zero_state, zero_state, abar_re, abar_im, bcat, ccat, sd, sgw, sgb,
                             tt=min(128, seq), seq_major=True)
            hp = _mix_ffn(hp, ya.reshape(bsz * seq, D_A), yb.reshape(bsz * seq, D_B), woa, wob, g_ffn,
                          wg, wu, wd)
            conv_p.append(cb_new)
            re_p.append(hr.reshape(bsz, S5_G, S5_P))
            im_p.append(hi.reshape(bsz, S5_G, S5_P))

            proj = _norm_matmul(hs, g_mix, w_in, tn=w_in.shape[1])
            ya, nb_t = _conv_sample(proj, jnp.transpose(state_conv[j].astype(F32), (1, 0, 2)), cw, cb, clg, clb)
            yb_t, hr, hi = _s5(proj[:, 2 * D_A:].reshape(1, n_s, D_B),
                               state_ssm_re[j].astype(F32).reshape(n_s, S5_STATE),
                               state_ssm_im[j].astype(F32).reshape(n_s, S5_STATE),
                               abar_re, abar_im, bcat, ccat, sd, sgw, sgb, tt=1, seq_major=False)
            hs = _mix_ffn(hs, ya, yb_t.reshape(n_s, D_B), woa, wob, g_ffn, wg, wu, wd)
            conv_s.append(jnp.transpose(nb_t, (1, 0, 2)))
            re_s.append(hr.reshape(n_s, S5_G, S5_P))
            im_s.append(hi.reshape(n_s, S5_G, S5_P))
        else:
            w_in = hg_w_in[j].astype(BF16)
            wo = hg_w_out[j].astype(BF16)
            lb, gn = row(lbs[j]), row(hg_gnorm[j])
            router_pad = jnp.pad(moe_router[j].astype(F32), ((0, 0), (0, LANES - N_EXP)))
            r_hi = router_pad.astype(BF16)
            r_lo = (router_pad - r_hi.astype(F32)).astype(BF16)

            proj = _norm_matmul(hp, g_mix, w_in, tn=1024, out_dtype=BF16)
            o, s_new = _hgrn_prompt(proj.reshape(bsz, seq, -1), lb, gn)
            h1_p, x_all, ids_p, gts_p = _mix_router(hp, o.reshape(bsz * seq, d), wo, g_ffn, r_hi, r_lo,
                                                    x_all, 0)
            hg_p.append(s_new)

            proj = _norm_matmul(hs, g_mix, w_in, tn=1024)
            o, s_new = _hgrn_sample(proj, state_hgrn[j].astype(F32), lb, gn)
            h1_s, x_all, ids_s, gts_s = _mix_router(hs, o, wo, g_ffn, r_hi, r_lo, x_all, bsz * seq)
            hg_s.append(s_new)

            n_rows = (2 * x_all.shape[0] + N_EXP * (MOE_TM - 1)) // MOE_TM * MOE_TM
            hp, hs = _moe(h1_p, h1_s, x_all, ids_p, ids_s, gts_p, gts_s, jnp.zeros((n_rows, d), F32),
                          moe_wg, moe_wu, moe_wd, j)

    g_fin = row(norm_final)
    y_prompt = _final_norm(hp, g_fin).reshape(bsz, seq, d)
    y_sample = _final_norm(hs, g_fin).reshape(n_s, 1, d)
    return (y_prompt, y_sample, jnp.stack(conv_p), jnp.stack(re_p), jnp.stack(im_p), jnp.stack(hg_p),
            jnp.stack(conv_s), jnp.stack(re_s), jnp.stack(im_s), jnp.stack(hg_s))
```

```python
import functools
import math

import jax
import jax.numpy as jnp
from jax import lax
from jax.experimental import pallas as pl
from jax.experimental.pallas import tpu as pltpu

F32 = jnp.float32
BF16 = jnp.bfloat16
I32 = jnp.int32

EPS = 1e-6
LN_EPS = 1e-5
FORGET_FLOOR = 1e-30

D_MODEL = 1024
D_A = 512
D_B = 512
CONV_W = 31
CONV_HALO = 32
S5_G = 32
S5_P = 64
S5_GH = 16
S5_STATE = S5_G * S5_P
S5_CHUNKS = 4
HG_HEADS = 8
HG_K = 128
HG_V = 128
HG_CHUNK = 64
D_FF = 2752
D_FF_PAD = 2816
N_EXP = 8
D_EXP = 3584
LANES = 128
VMEM_LIMIT_BYTES = 56 * 1024 * 1024
EXP_RANGE = 80.0

MOE_TM = 1024


def _cparams(*sem):
    return pltpu.CompilerParams(dimension_semantics=sem, vmem_limit_bytes=VMEM_LIMIT_BYTES)


def _sigmoid(x):
    return 1.0 / (1.0 + jnp.exp(-x))


def _silu(x):
    return x * _sigmoid(x)


def _rms(x, g):
    return x * lax.rsqrt(jnp.mean(x * x, axis=-1, keepdims=True) + EPS) * g


def _bdot(a, b):
    return jnp.dot(a.astype(BF16), b.astype(BF16), preferred_element_type=F32)


def _norm_matmul_kernel(x_ref, g_ref, w_ref, o_ref, xn_ref):
    @pl.when(pl.program_id(1) == 0)
    def _():
        xn_ref[...] = _rms(x_ref[...], g_ref[...]).astype(BF16)

    o_ref[...] = jnp.dot(xn_ref[...], w_ref[...], preferred_element_type=F32).astype(o_ref.dtype)


def _norm_matmul(x, g, w, tn, out_dtype=F32):
    n, d = x.shape
    nout = w.shape[1]
    tm = min(1024, n)
    return pl.pallas_call(
        _norm_matmul_kernel,
        grid=(n // tm, nout // tn),
        in_specs=[
            pl.BlockSpec((tm, d), lambda i, j: (i, 0)),
            pl.BlockSpec((1, d), lambda i, j: (0, 0)),
            pl.BlockSpec((d, tn), lambda i, j: (0, j)),
        ],
        out_specs=pl.BlockSpec((tm, tn), lambda i, j: (i, j)),
        out_shape=jax.ShapeDtypeStruct((n, nout), out_dtype),
        scratch_shapes=[pltpu.VMEM((tm, d), BF16)],
        compiler_params=_cparams("parallel", "arbitrary"),
        name="norm_matmul",
    )(x, g, w)


def _conv_post(y, b, lg, lb):
    y = y + b
    mu = jnp.mean(y, axis=-1, keepdims=True)
    yc = y - mu
    var = jnp.mean(yc * yc, axis=-1, keepdims=True)
    return _silu(yc * lax.rsqrt(var + LN_EPS) * lg + lb)


def _conv_prompt_kernel(av_ref, ag_ref, w_ref, b_ref, lg_ref, lb_ref, y_ref, nb_ref, ubuf, *, tt, rb):
    t = pl.program_id(1)

    @pl.when(t == 0)
    def _():
        ubuf[0:CONV_HALO, :] = jnp.zeros((CONV_HALO, D_A), F32)

    @pl.when(t > 0)
    def _():
        ubuf[0:CONV_HALO, :] = ubuf[tt:tt + CONV_HALO, :]

    ubuf[CONV_HALO:CONV_HALO + tt, :] = av_ref[0].astype(F32) * _sigmoid(ag_ref[0].astype(F32))
    off = CONV_HALO - (CONV_W - 1)
    span = rb + CONV_HALO
    for r in range(tt // rb):
        win = ubuf[r * rb:r * rb + span, :]
        acc = jnp.zeros((rb, D_A), F32)
        for p in range(8):
            wp = win if p == 0 else pltpu.roll(win, span - p, 0)
            for q in range(CONV_HALO // 8 + 1):
                k = 8 * q + p - off
                if 0 <= k < CONV_W:
                    acc = acc + w_ref[k:k + 1, :] * wp[8 * q:8 * q + rb, :]
        y_ref[0, r * rb:(r + 1) * rb, :] = _conv_post(acc, b_ref[...], lg_ref[...], lb_ref[...]).astype(y_ref.dtype)

    @pl.when(t == pl.num_programs(1) - 1)
    def _():
        nb_ref[0] = ubuf[tt + off:tt + CONV_HALO, :]


def _conv_prompt(proj3, w, b, lg, lb):
    bsz, t, _ = proj3.shape
    tt = min(256, t)
    kern = functools.partial(_conv_prompt_kernel, tt=tt, rb=32)
    vec = pl.BlockSpec((1, D_A), lambda i, j: (0, 0))
    return pl.pallas_call(
        kern,
        grid=(bsz, t // tt),
        in_specs=[
            pl.BlockSpec((1, tt, D_A), lambda i, j: (i, j, 0)),
            pl.BlockSpec((1, tt, D_A), lambda i, j: (i, j, 1)),
            pl.BlockSpec((CONV_W, D_A), lambda i, j: (0, 0)),
            vec, vec, vec,
        ],
        out_specs=[
            pl.BlockSpec((1, tt, D_A), lambda i, j: (i, j, 0)),
            pl.BlockSpec((1, CONV_W - 1, D_A), lambda i, j: (i, 0, 0)),
        ],
        out_shape=[
            jax.ShapeDtypeStruct((bsz, t, D_A), BF16),
            jax.ShapeDtypeStruct((bsz, CONV_W - 1, D_A), F32),
        ],
        scratch_shapes=[pltpu.VMEM((CONV_HALO + tt, D_A), F32)],
        compiler_params=_cparams("parallel", "arbitrary"),
        name="conv_prompt",
    )(proj3, proj3, w, b, lg, lb)


def _conv_sample_kernel(av_ref, ag_ref, buf_ref, w_ref, b_ref, lg_ref, lb_ref, y_ref, nb_ref):
    u = av_ref[...] * _sigmoid(ag_ref[...])
    acc = w_ref[CONV_W - 1:CONV_W, :] * u
    for k in range(CONV_W - 1):
        acc = acc + w_ref[k:k + 1, :] * buf_ref[k]
    y_ref[...] = _conv_post(acc, b_ref[...], lg_ref[...], lb_ref[...]).astype(y_ref.dtype)
    for k in range(CONV_W - 2):
        nb_ref[k] = buf_ref[k + 1]
    nb_ref[CONV_W - 2] = u


def _conv_sample(proj, buf_t, w, b, lg, lb):
    n = proj.shape[0]
    nbk = min(32, n)
    vec = pl.BlockSpec((1, D_A), lambda i: (0, 0))
    return pl.pallas_call(
        _conv_sample_kernel,
        grid=(n // nbk,),
        in_specs=[
            pl.BlockSpec((nbk, D_A), lambda i: (i, 0)),
            pl.BlockSpec((nbk, D_A), lambda i: (i, 1)),
            pl.BlockSpec((CONV_W - 1, nbk, D_A), lambda i: (0, i, 0)),
            pl.BlockSpec((CONV_W, D_A), lambda i: (0, 0)),
            vec, vec, vec,
        ],
        out_specs=[
            pl.BlockSpec((nbk, D_A), lambda i: (i, 0)),
            pl.BlockSpec((CONV_W - 1, nbk, D_A), lambda i: (0, i, 0)),
        ],
        out_shape=[
            jax.ShapeDtypeStruct((n, D_A), BF16),
            jax.ShapeDtypeStruct((CONV_W - 1, n, D_A), F32),
        ],
        compiler_params=_cparams("parallel"),
        name="conv_sample",
    )(proj, proj, buf_t, w, b, lg, lb)


def _s5_kernel(u_ref, h0r_ref, h0i_ref, ar_ref, ai_ref, bcat_ref, ccat_ref, d_ref, gw_ref, gb_ref,
               y_ref, hr_out, hi_out, sre, sim, cre, cim, rt, *, tt, nb, lw, seq_major):
    i = pl.program_id(0)
    m = tt * nb
    cw = S5_STATE // S5_CHUNKS

    @pl.when(i == 0)
    def _():
        cre[...] = h0r_ref[...]
        cim[...] = h0i_ref[...]

    if seq_major:
        for b in range(nb):
            rt[:, b, :] = u_ref[b].astype(F32)
        u = rt[...].reshape(m, D_B)
    else:
        u = u_ref[...].reshape(m, D_B)
    ub = u.astype(BF16)
    for c in range(S5_CHUNKS):
        bu = jnp.dot(ub[:, c * LANES:(c + 1) * LANES], bcat_ref[c], preferred_element_type=F32)
        sre[:, :, c * cw:(c + 1) * cw] = bu[:, :cw].reshape(tt, nb, cw)
        sim[:, :, c * cw:(c + 1) * cw] = bu[:, cw:].reshape(tt, nb, cw)

    for c in range(S5_STATE // lw):
        ls = slice(c * lw, (c + 1) * lw)
        ar = jnp.broadcast_to(ar_ref[:, ls], (nb, lw))
        ai = jnp.broadcast_to(ai_ref[:, ls], (nb, lw))

        def body(t, carry, ls=ls, ar=ar, ai=ai):
            hr, hi = carry
            nr = ar * hr - ai * hi + sre[t, :, ls]
            ni = ar * hi + ai * hr + sim[t, :, ls]
            sre[t, :, ls] = nr
            sim[t, :, ls] = ni
            return nr, ni

        hr, hi = lax.fori_loop(0, tt, body, (cre[:, ls], cim[:, ls]), unroll=min(tt, 8))
        cre[:, ls] = hr
        cim[:, ls] = hi

    hre = sre[...].reshape(m, S5_STATE)
    him = sim[...].reshape(m, S5_STATE)
    ys = []
    for c in range(S5_CHUNKS):
        hcat = jnp.concatenate([hre[:, c * cw:(c + 1) * cw], him[:, c * cw:(c + 1) * cw]], axis=1)
        ys.append(jnp.dot(hcat.astype(BF16), ccat_ref[c], preferred_element_type=F32))
    y = jnp.concatenate(ys, axis=1) + d_ref[...] * u
    y = 0.5 * y * (1.0 + lax.erf(y * (1.0 / math.sqrt(2.0))))
    z = jnp.dot(y.astype(BF16), gw_ref[...], preferred_element_type=F32) + gb_ref[...]
    out = (y * _sigmoid(z)).reshape(tt, nb, D_B)
    if seq_major:
        rt[...] = out
        for b in range(nb):
            y_ref[b] = rt[:, b, :].astype(y_ref.dtype)
    else:
        y_ref[...] = out.astype(y_ref.dtype)

    @pl.when(i == pl.num_programs(0) - 1)
    def _():
        hr_out[...] = cre[...]
        hi_out[...] = cim[...]


def _s5(u, h0r, h0i, abar_re, abar_im, bcat, ccat, d, glu_w, glu_b, tt, seq_major):
    if seq_major:
        nb, t, c = u.shape
        u_spec = pl.BlockSpec((nb, tt, D_B), lambda i: (0, i, c // D_B - 1))
        y_spec = pl.BlockSpec((nb, tt, D_B), lambda i: (0, i, 0))
        y_shape = (nb, t, D_B)
    else:
        t, nb, _ = u.shape
        u_spec = pl.BlockSpec((tt, nb, D_B), lambda i: (i, 0, 0))
        y_spec = u_spec
        y_shape = (t, nb, D_B)
    lw = max(LANES, min(512, 8 * 1024 // nb))
    kern = functools.partial(_s5_kernel, tt=tt, nb=nb, lw=lw, seq_major=seq_major)
    full = lambda shape: pl.BlockSpec(shape, lambda i: (0,) * len(shape))
    return pl.pallas_call(
        kern,
        grid=(t // tt,),
        in_specs=[
            u_spec,
            full((nb, S5_STATE)), full((nb, S5_STATE)),
            full((1, S5_STATE)), full((1, S5_STATE)),
            full(bcat.shape), full(ccat.shape),
            full((1, D_B)), full((D_B, D_B)), full((1, D_B)),
        ],
        out_specs=[
            y_spec,
            full((nb, S5_STATE)), full((nb, S5_STATE)),
        ],
        out_shape=[
            jax.ShapeDtypeStruct(y_shape, BF16),
            jax.ShapeDtypeStruct((nb, S5_STATE), F32),
            jax.ShapeDtypeStruct((nb, S5_STATE), F32),
        ],
        scratch_shapes=[
            pltpu.VMEM((tt, nb, S5_STATE), F32), pltpu.VMEM((tt, nb, S5_STATE), F32),
            pltpu.VMEM((nb, S5_STATE), F32), pltpu.VMEM((nb, S5_STATE), F32),
            pltpu.VMEM((tt, nb, D_B), F32),
        ],
        compiler_params=_cparams("arbitrary"),
        name="s5",
    )(u, h0r, h0i, abar_re, abar_im, bcat, ccat, d, glu_w, glu_b)


def _s5_params(a_re, a_im, log_dt, b_re, b_im, c_re, c_im):
    dt = jnp.exp(log_dt)[:, None]
    mag = jnp.exp(dt * a_re)
    ang = dt * a_im
    abar_re, abar_im = mag * jnp.cos(ang), mag * jnp.sin(ang)
    den = a_re * a_re + a_im * a_im
    nr, ni = abar_re - 1.0, abar_im
    coef_re = (nr * a_re + ni * a_im) / den
    coef_im = (ni * a_re - nr * a_im) / den
    bbar_re = coef_re[..., None] * b_re - coef_im[..., None] * b_im
    bbar_im = coef_re[..., None] * b_im + coef_im[..., None] * b_re
    gpc = S5_G // S5_CHUNKS
    eye = jnp.eye(gpc, dtype=F32)

    def bblk(x):
        x = x.reshape(S5_CHUNKS, gpc, S5_P, S5_GH)
        return jnp.einsum("cgph,gk->cghkp", x, eye).reshape(S5_CHUNKS, gpc * S5_GH, gpc * S5_P)

    def cblk(x):
        x = x.reshape(S5_CHUNKS, gpc, S5_GH, S5_P)
        return jnp.einsum("cghp,gk->cgpkh", x, eye).reshape(S5_CHUNKS, gpc * S5_P, gpc * S5_GH)

    bcat = jnp.concatenate([bblk(bbar_re), bblk(bbar_im)], axis=2).astype(BF16)
    ccat = jnp.concatenate([cblk(c_re), -cblk(c_im)], axis=1).astype(BF16)
    return abar_re.reshape(1, S5_STATE), abar_im.reshape(1, S5_STATE), bcat, ccat


def _hg_gates(q, f, lb):
    qf = _silu(q)
    sig = _sigmoid(f)
    forget = lb + (1.0 - lb) * sig
    logg = jnp.log(jnp.maximum(forget, FORGET_FLOOR))
    kf = (1.0 - lb) * (1.0 - sig)
    return qf, logg, kf


def _hg_out(o, g, gn):
    return o * lax.rsqrt(jnp.mean(o * o, axis=-1, keepdims=True) + EPS) * gn * _silu(g)


def _cumsum_rows(x, tri):
    hi = x.astype(BF16)
    r1 = x - hi.astype(F32)
    mid = r1.astype(BF16)
    lo = (r1 - mid.astype(F32)).astype(BF16)
    dot = lambda p: jnp.dot(tri, p, preferred_element_type=F32)
    return dot(hi) + dot(mid) + dot(lo)


def _hgrn_prompt_kernel(xc_ref, xn_ref, nm_ref, w_ref, lb_ref, gn_ref, o_ref, s_out, st, obuf, pbuf, *, tt):
    t = pl.program_id(1)
    L = HG_CHUNK
    dc = HG_HEADS * HG_K
    step = pl.program_id(0) * pl.num_programs(1) + t
    slot = step % 2

    def project(x_ref):
        xn = _rms(x_ref[0], nm_ref[...]).astype(BF16)
        return jnp.dot(xn, w_ref[...], preferred_element_type=F32).astype(BF16)

    def project_next():
        pbuf[1 - slot] = project(xn_ref)

    @pl.when(step == 0)
    def _():
        pbuf[0] = project(xc_ref)

    @pl.when(t == 0)
    def _():
        st[...] = jnp.zeros(st.shape, F32)

    q_of = lambda rs: pbuf[slot, rs, 0:dc]
    f_of = lambda rs: pbuf[slot, rs, dc:2 * dc]
    v_of = lambda rs: pbuf[slot, rs, 2 * dc:3 * dc]
    g_of = lambda rs, hs: pbuf[slot, rs, 3 * dc + hs.start:3 * dc + hs.stop]

    row = lax.broadcasted_iota(I32, (L, L), 0)
    col = lax.broadcasted_iota(I32, (L, L), 1)
    causal = row >= col
    tri = causal.astype(BF16)
    nt = (((1,), (1,)), ((), ()))
    tn = (((0,), (0,)), ((), ()))
    chunks = []
    spread = None
    for c in range(tt // L):
        rs = slice(c * L, (c + 1) * L)
        q, logg, k = _hg_gates(q_of(rs).astype(F32), f_of(rs).astype(F32), lb_ref[...])
        vb = v_of(rs)
        gc = _cumsum_rows(logg, tri)
        gmid = gc[L // 2 - 1:L // 2, :]
        glast = gc[L - 1:L, :]
        sp = jnp.maximum(gc[0:1, :] - gmid, gmid - glast)
        spread = sp if spread is None else jnp.maximum(spread, sp)
        chunks.append((rs, q, logg, k, vb, gc, gmid, glast))
    fast = jnp.max(spread) <= EXP_RANGE

    @pl.when(fast)
    def _():
        for rs, q, logg, k, vb, gc, gmid, glast in chunks:
            qe = (q * jnp.exp(gc - gmid)).astype(BF16)
            ke = (k * jnp.exp(gmid - gc)).astype(BF16)
            qg = (q * jnp.exp(gc)).astype(BF16)
            kd = (k * jnp.exp(glast - gc)).astype(BF16)
            dlast = jnp.exp(glast)
            for h in range(HG_HEADS):
                hs = slice(h * HG_K, (h + 1) * HG_K)
                sc = lax.dot_general(qe[:, hs], ke[:, hs], nt, preferred_element_type=F32)
                sc = jnp.where(causal, sc, 0.0).astype(BF16)
                s_t = st[h]
                o = (jnp.dot(sc, vb[:, hs], preferred_element_type=F32)
                     + lax.dot_general(qg[:, hs], s_t.astype(BF16), nt, preferred_element_type=F32))
                st[h] = dlast[:, hs] * s_t + lax.dot_general(vb[:, hs], kd[:, hs], tn,
                                                             preferred_element_type=F32)
                o_ref[0, rs, hs] = _hg_out(o, g_of(rs, hs).astype(F32), gn_ref[...]).astype(o_ref.dtype)
        project_next()

    @pl.when(jnp.logical_not(fast))
    def _():
        for rs, q, logg, k, vb, gc, gmid, glast in chunks:
            dec = jnp.exp(logg)
            kb = k.astype(BF16)
            rsel = lax.broadcasted_iota(I32, (L, 1), 0)
            obuf[...] = jnp.zeros(obuf.shape, F32)

            def step(t, carry):
                sel = rsel == t
                g_t = jnp.sum(jnp.where(sel, dec, 0.0), axis=0, keepdims=True)
                v_t = jnp.where(sel, vb, jnp.zeros_like(vb))
                q_t = jnp.where(sel, q, 0.0).astype(BF16)
                for h in range(HG_HEADS):
                    hs = slice(h * HG_K, (h + 1) * HG_K)
                    s_new = g_t[:, hs] * st[h] + lax.dot_general(v_t[:, hs], kb[:, hs], tn,
                                                                 preferred_element_type=F32)
                    st[h] = s_new
                    obuf[:, hs] += lax.dot_general(q_t[:, hs], s_new.astype(BF16), nt,
                                                   preferred_element_type=F32)
                return carry

            lax.fori_loop(0, L, step, 0)
            for h in range(HG_HEADS):
                hs = slice(h * HG_K, (h + 1) * HG_K)
                o_ref[0, rs, hs] = _hg_out(obuf[:, hs], g_of(rs, hs).astype(F32),
                                           gn_ref[...]).astype(o_ref.dtype)
        project_next()

    @pl.when(t == pl.num_programs(1) - 1)
    def _():
        for h in range(HG_HEADS):
            s_out[0, h] = st[h].T


def _hgrn_prompt(h3, g_mix, w_in, lb, gn):
    bsz, t, d = h3.shape
    tt = min(256, t)
    nt = t // tt
    dc = HG_HEADS * HG_K
    kern = functools.partial(_hgrn_prompt_kernel, tt=tt)

    def nxt(i, j):
        f = jnp.minimum(i * nt + j + 1, bsz * nt - 1)
        return (f // nt, f % nt, 0)

    return pl.pallas_call(
        kern,
        grid=(bsz, nt),
        in_specs=[pl.BlockSpec((1, tt, d), lambda i, j: (i, j, 0)),
                  pl.BlockSpec((1, tt, d), nxt),
                  pl.BlockSpec((1, d), lambda i, j: (0, 0)),
                  pl.BlockSpec((d, 4 * dc), lambda i, j: (0, 0)),
                  pl.BlockSpec((1, dc), lambda i, j: (0, 0)),
                  pl.BlockSpec((1, HG_V), lambda i, j: (0, 0))],
        out_specs=[
            pl.BlockSpec((1, tt, dc), lambda i, j: (i, j, 0)),
            pl.BlockSpec((1, HG_HEADS, HG_K, HG_V), lambda i, j: (i, 0, 0, 0)),
        ],
        out_shape=[
            jax.ShapeDtypeStruct((bsz, t, dc), BF16),
            jax.ShapeDtypeStruct((bsz, HG_HEADS, HG_K, HG_V), F32),
        ],
        scratch_shapes=[pltpu.VMEM((HG_HEADS, HG_V, HG_K), F32), pltpu.VMEM((HG_CHUNK, dc), F32),
                        pltpu.VMEM((2, tt, 4 * dc), BF16)],
        compiler_params=_cparams("arbitrary", "arbitrary"),
        name="hgrn_prompt",
    )(h3, h3, g_mix, w_in, lb, gn)


def _hgrn_sample_kernel(q_ref, f_ref, v_ref, g_ref, s_ref, lb_ref, gn_ref, o_ref, s_out, *, nbk):
    for h in range(HG_HEADS):
        hs = slice(h * HG_K, (h + 1) * HG_K)
        q, logg, k = _hg_gates(q_ref[:, hs], f_ref[:, hs], lb_ref[:, hs])
        dec = jnp.exp(logg)
        v = v_ref[:, hs]
        rows = []
        for n in range(nbk):
            col = lambda x: jnp.broadcast_to(x[n:n + 1, :], (HG_K, HG_K)).T
            s_new = col(dec) * s_ref[0, n, h] + col(k) * v[n:n + 1, :]
            s_out[n, h] = s_new
            rows.append(jnp.sum(col(q) * s_new, axis=0, keepdims=True))
        o = jnp.concatenate(rows, axis=0)
        o_ref[:, hs] = _hg_out(o, g_ref[:, hs], gn_ref[...]).astype(o_ref.dtype)


def _hgrn_sample(proj, s_all, layer, lb, gn):
    n = proj.shape[0]
    nbk = 8
    dc = HG_HEADS * HG_K
    kern = functools.partial(_hgrn_sample_kernel, nbk=nbk)
    blk = lambda c: pl.BlockSpec((nbk, dc), lambda i, c=c: (i, c))
    return pl.pallas_call(
        kern,
        grid=(n // nbk,),
        in_specs=[blk(0), blk(1), blk(2), blk(3),
                  pl.BlockSpec((1, nbk, HG_HEADS, HG_K, HG_V), lambda i: (layer, i, 0, 0, 0)),
                  pl.BlockSpec((1, dc), lambda i: (0, 0)),
                  pl.BlockSpec((1, HG_V), lambda i: (0, 0))],
        out_specs=[pl.BlockSpec((nbk, dc), lambda i: (i, 0)),
                   pl.BlockSpec((nbk, HG_HEADS, HG_K, HG_V), lambda i: (i, 0, 0, 0))],
        out_shape=[
            jax.ShapeDtypeStruct((n, dc), BF16),
            jax.ShapeDtypeStruct(s_all.shape[1:], F32),
        ],
        compiler_params=_cparams("parallel"),
        name="hgrn_sample",
    )(proj, proj, proj, proj, s_all, lb, gn)


def _mix_ffn_kernel(h_ref, ya_ref, yb_ref, woa_ref, wob_ref, g_ref, wg_ref, wu_ref, wd_ref, o_ref,
                    h1_ref, xn_ref, acc_ref):
    k = pl.program_id(1)

    @pl.when(k == 0)
    def _():
        h1 = (h_ref[...] + jnp.dot(ya_ref[...], woa_ref[...], preferred_element_type=F32)
              + jnp.dot(yb_ref[...], wob_ref[...], preferred_element_type=F32))
        h1_ref[...] = h1
        xn_ref[...] = _rms(h1, g_ref[...]).astype(BF16)
        acc_ref[...] = jnp.zeros(acc_ref.shape, F32)

    xn = xn_ref[...]
    a = jnp.dot(xn, wg_ref[...], preferred_element_type=F32)
    b = jnp.dot(xn, wu_ref[...], preferred_element_type=F32)
    acc_ref[...] += jnp.dot((_silu(a) * b).astype(BF16), wd_ref[...], preferred_element_type=F32)

    @pl.when(k == pl.num_programs(1) - 1)
    def _():
        o_ref[...] = h1_ref[...] + acc_ref[...]


def _mix_ffn(h, ya, yb, woa, wob, g, wg, wu, wd):
    n, d = h.shape
    tm = min(512, n)
    tf = D_FF_PAD // 2
    row = lambda w: pl.BlockSpec((tm, w), lambda i, k: (i, 0))
    return pl.pallas_call(
        _mix_ffn_kernel,
        grid=(n // tm, D_FF_PAD // tf),
        in_specs=[
            row(d), row(D_A), row(D_B),
            pl.BlockSpec((D_A, d), lambda i, k: (0, 0)),
            pl.BlockSpec((D_B, d), lambda i, k: (0, 0)),
            pl.BlockSpec((1, d), lambda i, k: (0, 0)),
            pl.BlockSpec((d, tf), lambda i, k: (0, k)),
            pl.BlockSpec((d, tf), lambda i, k: (0, k)),
            pl.BlockSpec((tf, d), lambda i, k: (k, 0)),
        ],
        out_specs=row(d),
        out_shape=jax.ShapeDtypeStruct((n, d), F32),
        scratch_shapes=[pltpu.VMEM((tm, d), F32), pltpu.VMEM((tm, d), BF16), pltpu.VMEM((tm, d), F32)],
        compiler_params=_cparams("parallel", "arbitrary"),
        name="mix_ffn",
    )(h, ya, yb, woa, wob, g, wg, wu, wd)


def _mix_router_kernel(h_ref, y_ref, wo_ref, g_ref, rh_ref, rl_ref, *rest):
    h1_ref, xn_ref, ids_ref, gts_ref = rest[-4:]
    h1 = h_ref[...] + jnp.dot(y_ref[...], wo_ref[...], preferred_element_type=F32)
    h1_ref[...] = h1
    xn = _rms(h1, g_ref[...])
    xn_ref[...] = xn
    xh = xn.astype(BF16)
    xl = (xn - xh.astype(F32)).astype(BF16)
    dot = lambda a, b: jnp.dot(a, b, preferred_element_type=F32)
    logits = dot(xh, rh_ref[...]) + (dot(xh, rl_ref[...]) + dot(xl, rh_ref[...]))
    lane = lax.broadcasted_iota(I32, logits.shape, 1)
    neg = jnp.float32(-jnp.inf)
    logits = jnp.where(lane < N_EXP, logits, neg)
    m1 = jnp.max(logits, axis=-1, keepdims=True)
    i1 = jnp.min(jnp.where(logits == m1, lane, LANES), axis=-1, keepdims=True)
    rest = jnp.where(lane == i1, neg, logits)
    m2 = jnp.max(rest, axis=-1, keepdims=True)
    i2 = jnp.min(jnp.where(rest == m2, lane, LANES), axis=-1, keepdims=True)
    e2 = jnp.exp(m2 - m1)
    g1 = 1.0 / (1.0 + e2)
    g2 = e2 / (1.0 + e2)
    ids_ref[...] = jnp.where(lane == 0, i1, jnp.where(lane == 1, i2, 0))
    gts_ref[...] = jnp.where(lane == 0, g1, jnp.where(lane == 1, g2, 0.0))


def _mix_router(h, y, wo, g, r_hi, r_lo, xn_all, row0):
    n, d = h.shape
    tm = min(1024, n)
    row = lambda w: pl.BlockSpec((tm, w), lambda i: (i, 0))
    return pl.pallas_call(
        _mix_router_kernel,
        grid=(n // tm,),
        in_specs=[row(d), row(d),
                  pl.BlockSpec((d, d), lambda i: (0, 0)),
                  pl.BlockSpec((1, d), lambda i: (0, 0)),
                  pl.BlockSpec((d, LANES), lambda i: (0, 0)),
                  pl.BlockSpec((d, LANES), lambda i: (0, 0)),
                  pl.BlockSpec(memory_space=pl.ANY)],
        out_specs=[row(d), pl.BlockSpec((tm, d), lambda i: (i + row0 // tm, 0)), row(LANES), row(LANES)],
        out_shape=[
            jax.ShapeDtypeStruct((n, d), F32),
            jax.ShapeDtypeStruct(xn_all.shape, F32),
            jax.ShapeDtypeStruct((n, LANES), I32),
            jax.ShapeDtypeStruct((n, LANES), F32),
        ],
        input_output_aliases={6: 1},
        compiler_params=_cparams("parallel"),
        name="mix_router",
    )(h, y, wo, g, r_hi, r_lo, xn_all)


def _wait_rows(src, dst, sem, copies):
    for _ in range(copies):
        pltpu.make_async_copy(src, dst, sem).wait()


def _dispatch_kernel(pos_ref, x_ref, xs_in, xs_hbm, sem, *, tt):
    del xs_in
    i = pl.program_id(0)

    def body(r, c):
        t = i * tt + r
        for j in range(2):
            pltpu.make_async_copy(x_ref.at[pl.ds(r, 1), :], xs_hbm.at[pl.ds(pos_ref[2 * t + j], 1), :],
                                  sem.at[0]).start()
        return c

    lax.fori_loop(0, tt, body, 0, unroll=8)
    _wait_rows(x_ref, xs_hbm.at[pl.ds(0, tt), :], sem.at[0], 2)


def _dispatch(x_all, pos, xs_zero):
    n, d = x_all.shape
    tt = 384 if n % 384 == 0 else LANES
    kern = functools.partial(_dispatch_kernel, tt=tt)
    grid_spec = pltpu.PrefetchScalarGridSpec(
        num_scalar_prefetch=1,
        grid=(n // tt,),
        in_specs=[pl.BlockSpec((tt, d), lambda i, p: (i, 0)), pl.BlockSpec(memory_space=pl.ANY)],
        out_specs=pl.BlockSpec(memory_space=pl.ANY),
        scratch_shapes=[pltpu.SemaphoreType.DMA((1,))],
    )
    return pl.pallas_call(
        kern,
        grid_spec=grid_spec,
        out_shape=jax.ShapeDtypeStruct(xs_zero.shape, F32),
        input_output_aliases={2: 0},
        compiler_params=_cparams("arbitrary"),
        name="moe_dispatch",
    )(pos, x_all, xs_zero)


def _expert_kernel(te_ref, nu_ref, xs_ref, wg_ref, wu_ref, wd_ref, ys_ref, xb, acc_ref):
    i = pl.program_id(0)
    k = pl.program_id(1)
    used = i < nu_ref[0]

    @pl.when(k == 0)
    def _():
        acc_ref[...] = jnp.zeros(acc_ref.shape, F32)
        xb[...] = xs_ref[...].astype(BF16)

    @pl.when(used)
    def _():
        x = xb[...]
        a = jnp.dot(x, wg_ref[0, 0].astype(BF16), preferred_element_type=F32)
        b = jnp.dot(x, wu_ref[0, 0].astype(BF16), preferred_element_type=F32)
        acc_ref[...] += jnp.dot((_silu(a) * b).astype(BF16), wd_ref[0, 0].astype(BF16),
                                preferred_element_type=F32)

    @pl.when(k == pl.num_programs(1) - 1)
    def _():
        ys_ref[...] = acc_ref[...]


def _experts(xs, tile_expert, n_used, wg, wu, wd, layer):
    n_rows = xs.shape[0]
    tm = MOE_TM
    nk = 7
    tk = D_EXP // nk

    def kk(i, k, nu):
        return jnp.where(i < nu[0], k, nk - 1)

    grid_spec = pltpu.PrefetchScalarGridSpec(
        num_scalar_prefetch=2,
        grid=(n_rows // tm, nk),
        in_specs=[
            pl.BlockSpec((tm, D_MODEL), lambda i, k, te, nu: (i, 0)),
            pl.BlockSpec((1, 1, D_MODEL, tk), lambda i, k, te, nu: (layer, te[i], 0, kk(i, k, nu))),
            pl.BlockSpec((1, 1, D_MODEL, tk), lambda i, k, te, nu: (layer, te[i], 0, kk(i, k, nu))),
            pl.BlockSpec((1, 1, tk, D_MODEL), lambda i, k, te, nu: (layer, te[i], kk(i, k, nu), 0)),
        ],
        out_specs=pl.BlockSpec((tm, D_MODEL), lambda i, k, te, nu: (i, 0)),
        scratch_shapes=[pltpu.VMEM((tm, D_MODEL), BF16), pltpu.VMEM((tm, D_MODEL), F32)],
    )
    return pl.pallas_call(
        _expert_kernel,
        grid_spec=grid_spec,
        out_shape=jax.ShapeDtypeStruct((n_rows, D_MODEL), F32),
        compiler_params=_cparams("arbitrary", "arbitrary"),
        name="moe_experts",
    )(tile_expert, n_used, xs, wg, wu, wd)


def _combine_kernel(pos_ref, h_ref, g_ref, ys_hbm, fin_ref, o_ref, buf, sem, *, tt, tok0, final):
    i = pl.program_id(0)

    def issue(tile, slot):
        def body(r, c):
            t = tok0 + tile * tt + r
            for j in range(2):
                pltpu.make_async_copy(ys_hbm.at[pl.ds(pos_ref[2 * t + j], 1), :],
                                      buf.at[slot, j, pl.ds(r, 1), :], sem.at[slot]).start()
            return c

        lax.fori_loop(0, tt, body, 0, unroll=8)

    @pl.when(i == 0)
    def _():
        issue(0, 0)

    @pl.when(i + 1 < pl.num_programs(0))
    def _():
        issue(i + 1, (i + 1) % 2)

    slot = i % 2
    _wait_rows(ys_hbm.at[pl.ds(0, tt), :], buf.at[slot, 0], sem.at[slot], 2)
    g = g_ref[...]
    out = h_ref[...] + (g[:, 0:1] * buf[slot, 0] + g[:, 1:2] * buf[slot, 1])
    o_ref[...] = _rms(out, fin_ref[...]) if final else out


def _combine(h1, gts, pos, ys, tok0, fin_g, final):
    n, d = h1.shape
    tt = min(512, n)
    kern = functools.partial(_combine_kernel, tt=tt, tok0=tok0, final=final)
    grid_spec = pltpu.PrefetchScalarGridSpec(
        num_scalar_prefetch=1,
        grid=(n // tt,),
        in_specs=[pl.BlockSpec((tt, d), lambda i, p: (i, 0)),
                  pl.BlockSpec((tt, LANES), lambda i, p: (i, 0)),
                  pl.BlockSpec(memory_space=pl.ANY),
                  pl.BlockSpec((1, d), lambda i, p: (0, 0))],
        out_specs=pl.BlockSpec((tt, d), lambda i, p: (i, 0)),
        scratch_shapes=[pltpu.VMEM((2, 2, tt, d), F32), pltpu.SemaphoreType.DMA((2,))],
    )
    return pl.pallas_call(
        kern,
        grid_spec=grid_spec,
        out_shape=jax.ShapeDtypeStruct((n, d), F32),
        compiler_params=_cparams("arbitrary"),
        name="moe_combine",
    )(pos, h1, gts, ys, fin_g)


def _moe(h1_p, h1_s, x_all, ids_p, ids_s, gts_p, gts_s, xs_zero, wg, wu, wd, layer, fin_g, final):
    n_p = h1_p.shape[0]
    tm = MOE_TM
    n_tiles = xs_zero.shape[0] // tm
    e = jnp.concatenate([ids_p[:, :2], ids_s[:, :2]], axis=0)
    hit = jnp.sum((e[:, :, None] == jnp.arange(N_EXP, dtype=I32)).astype(I32), axis=1)
    cum = jnp.cumsum(hit, axis=0)
    rank = cum - hit
    counts = cum[-1]
    padded = ((counts + tm - 1) // tm) * tm
    ends = jnp.cumsum(padded)
    offs = ends - padded
    pos = (offs[e] + jnp.take_along_axis(rank, e, axis=1)).reshape(-1).astype(I32)
    n_used = (ends[-1] // tm).astype(I32).reshape(1)
    tile_start = jnp.arange(n_tiles, dtype=I32) * tm
    tile_expert = jnp.sum((ends[None, :] <= tile_start[:, None]).astype(I32), axis=1)
    tile_expert = jnp.minimum(tile_expert, N_EXP - 1).astype(I32)

    xs = _dispatch(x_all, pos, xs_zero)
    ys = _experts(xs, tile_expert, n_used, wg, wu, wd, layer)
    out_p = _combine(h1_p, gts_p, pos, ys, 0, fin_g, final)
    out_s = _combine(h1_s, gts_s, pos, ys, n_p, fin_g, final)
    return out_p, out_s


def _final_norm_kernel(x_ref, g_ref, o_ref):
    o_ref[...] = _rms(x_ref[...], g_ref[...])


def _final_norm(x, g):
    n, d = x.shape
    tm = min(1024, n)
    return pl.pallas_call(
        _final_norm_kernel,
        grid=(n // tm,),
        in_specs=[pl.BlockSpec((tm, d), lambda i: (i, 0)), pl.BlockSpec((1, d), lambda i: (0, 0))],
        out_specs=pl.BlockSpec((tm, d), lambda i: (i, 0)),
        out_shape=jax.ShapeDtypeStruct((n, d), F32),
        compiler_params=_cparams("parallel"),
        name="final_norm",
    )(x, g)


def kernel(x_prompt, x_sample, state_conv, state_ssm_re, state_ssm_im, state_hgrn, norm_mix, norm_ffn, norm_final, ab_w_in, ab_w_out, conv_w, conv_b, conv_ln_g, conv_ln_b, ssm_a_re, ssm_a_im, ssm_log_dt, ssm_b_re, ssm_b_im, ssm_c_re, ssm_c_im, ssm_d, ssm_glu_w, ssm_glu_b, hg_w_in, hg_lb_logits, hg_gnorm, hg_w_out, ffn_w_gate, ffn_w_up, ffn_w_down, moe_router, moe_w_gate, moe_w_up, moe_w_down):
    bsz, seq, d = x_prompt.shape
    n_s = x_sample.shape[0]
    depth = norm_mix.shape[0]
    hp = x_prompt.reshape(bsz * seq, d)
    hs = x_sample.reshape(n_s, d)

    lbs = jax.nn.softmax(hg_lb_logits.astype(F32), axis=0)
    lbs = jnp.cumsum(lbs, axis=0) - lbs[0:1]

    conv_p, re_p, im_p, hg_p = [], [], [], []
    conv_s, re_s, im_s, hg_s = [], [], [], []
    zero_state = jnp.zeros((bsz, S5_STATE), F32)
    x_all = jnp.zeros((bsz * seq + n_s, d), F32)
    moe_wg, moe_wu, moe_wd = moe_w_gate.astype(F32), moe_w_up.astype(F32), moe_w_down.astype(F32)
    row = lambda v: v.reshape(1, -1).astype(F32)
    g_fin = row(norm_final)

    for l in range(depth):
        j = l // 2
        g_mix, g_ffn = row(norm_mix[l]), row(norm_ffn[l])
        if l % 2 == 0:
            w_in = ab_w_in[j].astype(BF16)
            woa = ab_w_out[j, :D_A].astype(BF16)
            wob = ab_w_out[j, D_A:].astype(BF16)
            cw, cb, clg, clb = conv_w[j].astype(F32), row(conv_b[j]), row(conv_ln_g[j]), row(conv_ln_b[j])
            abar_re, abar_im, bcat, ccat = _s5_params(
                ssm_a_re[j].astype(F32), ssm_a_im[j].astype(F32), ssm_log_dt[j].astype(F32),
                ssm_b_re[j].astype(F32), ssm_b_im[j].astype(F32), ssm_c_re[j].astype(F32), ssm_c_im[j].astype(F32))
            sd, sgw, sgb = row(ssm_d[j]), ssm_glu_w[j].astype(BF16), row(ssm_glu_b[j])
            pad = ((0, 0), (0, D_FF_PAD - D_FF))
            wg = jnp.pad(ffn_w_gate[j], pad).astype(BF16)
            wu = jnp.pad(ffn_w_up[j], pad).astype(BF16)
            wd = jnp.pad(ffn_w_down[j], (pad[1], pad[0])).astype(BF16)

            proj = _norm_matmul(hp, g_mix, w_in, tn=w_in.shape[1], out_dtype=BF16)
            proj3 = proj.reshape(bsz, seq, -1)
            ya, cb_new = _conv_prompt(proj3, cw, cb, clg, clb)
            yb, hr, hi = _s5(proj3, zero_state, zero_state, abar_re, abar_im, bcat, ccat, sd, sgw, sgb,
                             tt=min(128, seq), seq_major=True)
            hp = _mix_ffn(hp, ya.reshape(bsz * seq, D_A), yb.reshape(bsz * seq, D_B), woa, wob, g_ffn,
                          wg, wu, wd)
            conv_p.append(cb_new)
            re_p.append(hr.reshape(bsz, S5_G, S5_P))
            im_p.append(hi.reshape(bsz, S5_G, S5_P))

            proj = _norm_matmul(hs, g_mix, w_in, tn=w_in.shape[1])
            ya, nb_t = _conv_sample(proj, jnp.transpose(state_conv[j].astype(F32), (1, 0, 2)), cw, cb, clg, clb)
            yb_t, hr, hi = _s5(proj[:, 2 * D_A:].reshape(1, n_s, D_B),
                               state_ssm_re[j].astype(F32).reshape(n_s, S5_STATE),
                               state_ssm_im[j].astype(F32).reshape(n_s, S5_STATE),
                               abar_re, abar_im, bcat, ccat, sd, sgw, sgb, tt=1, seq_major=False)
            hs = _mix_ffn(hs, ya, yb_t.reshape(n_s, D_B), woa, wob, g_ffn, wg, wu, wd)
            conv_s.append(jnp.transpose(nb_t, (1, 0, 2)))
            re_s.append(hr.reshape(n_s, S5_G, S5_P))
            im_s.append(hi.reshape(n_s, S5_G, S5_P))
        else:
            w_in = hg_w_in[j].astype(BF16)
            wo = hg_w_out[j].astype(BF16)
            lb, gn = row(lbs[j]), row(hg_gnorm[j])
            router_pad = jnp.pad(moe_router[j].astype(F32), ((0, 0), (0, LANES - N_EXP)))
            r_hi = router_pad.astype(BF16)
            r_lo = (router_pad - r_hi.astype(F32)).astype(BF16)

            o, s_new = _hgrn_prompt(hp.reshape(bsz, seq, d), g_mix, w_in, lb, gn)
            h1_p, x_all, ids_p, gts_p = _mix_router(hp, o.reshape(bsz * seq, d), wo, g_ffn, r_hi, r_lo,
                                                    x_all, 0)
            hg_p.append(s_new)

            proj = _norm_matmul(hs, g_mix, w_in, tn=1024)
            o, s_new = _hgrn_sample(proj, state_hgrn.astype(F32), j, lb, gn)
            h1_s, x_all, ids_s, gts_s = _mix_router(hs, o, wo, g_ffn, r_hi, r_lo, x_all, bsz * seq)
            hg_s.append(s_new)

            n_rows = (2 * x_all.shape[0] + N_EXP * (MOE_TM - 1)) // MOE_TM * MOE_TM
            hp, hs = _moe(h1_p, h1_s, x_all, ids_p, ids_s, gts_p, gts_s, jnp.zeros((n_rows, d), F32),
                          moe_wg, moe_wu, moe_wd, j, g_fin, l == depth - 1)

    if depth % 2 == 1:
        hp, hs = _final_norm(hp, g_fin), _final_norm(hs, g_fin)
    y_prompt = hp.reshape(bsz, seq, d)
    y_sample = hs.reshape(n_s, 1, d)
    return (y_prompt, y_sample, jnp.stack(conv_p), jnp.stack(re_p), jnp.stack(im_p), jnp.stack(hg_p),
            jnp.stack(conv_s), jnp.stack(re_s), jnp.stack(im_s), jnp.stack(hg_s))
```

```python
import functools
import math

import jax
import jax.numpy as jnp
from jax import lax
from jax.experimental import pallas as pl
from jax.experimental.pallas import tpu as pltpu

F32 = jnp.float32
BF16 = jnp.bfloat16
I32 = jnp.int32

EPS = 1e-6
LN_EPS = 1e-5
FORGET_FLOOR = 1e-30

D_MODEL = 1024
D_A = 512
D_B = 512
CONV_W = 31
CONV_HALO = 32
S5_G = 32
S5_P = 64
S5_GH = 16
S5_STATE = S5_G * S5_P
S5_CHUNKS = 4
HG_HEADS = 8
HG_K = 128
HG_V = 128
HG_CHUNK = 64
D_FF = 2752
D_FF_PAD = 2816
N_EXP = 8
D_EXP = 3584
LANES = 128
VMEM_LIMIT_BYTES = 56 * 1024 * 1024
EXP_RANGE = 80.0

MOE_TM = 1024


def _cparams(*sem):
    return pltpu.CompilerParams(dimension_semantics=sem, vmem_limit_bytes=VMEM_LIMIT_BYTES)


def _sigmoid(x):
    return 1.0 / (1.0 + jnp.exp(-x))


def _silu(x):
    return x * _sigmoid(x)


def _rms(x, g):
    return x * lax.rsqrt(jnp.mean(x * x, axis=-1, keepdims=True) + EPS) * g


def _bdot(a, b):
    return jnp.dot(a.astype(BF16), b.astype(BF16), preferred_element_type=F32)


def _pdot(a, b, precise):
    if precise:
        return jnp.dot(a.astype(F32), b.astype(F32), preferred_element_type=F32, precision=lax.Precision.HIGHEST)
    return jnp.dot(a.astype(BF16), b.astype(BF16), preferred_element_type=F32)


def _norm_matmul_kernel(x_ref, g_ref, w_ref, o_ref, xn_ref, *, precise):
    @pl.when(pl.program_id(1) == 0)
    def _():
        xn_ref[...] = _rms(x_ref[...], g_ref[...]).astype(xn_ref.dtype)

    o_ref[...] = _pdot(xn_ref[...], w_ref[...], precise).astype(o_ref.dtype)


def _norm_matmul(x, g, w, tn, out_dtype=F32):
    precise = w.dtype == F32
    n, d = x.shape
    nout = w.shape[1]
    tm = min(1024, n)
    return pl.pallas_call(
        functools.partial(_norm_matmul_kernel, precise=precise),
        grid=(n // tm, nout // tn),
        in_specs=[
            pl.BlockSpec((tm, d), lambda i, j: (i, 0)),
            pl.BlockSpec((1, d), lambda i, j: (0, 0)),
            pl.BlockSpec((d, tn), lambda i, j: (0, j)),
        ],
        out_specs=pl.BlockSpec((tm, tn), lambda i, j: (i, j)),
        out_shape=jax.ShapeDtypeStruct((n, nout), out_dtype),
        scratch_shapes=[pltpu.VMEM((tm, d), F32 if precise else BF16)],
        compiler_params=_cparams("parallel", "arbitrary"),
        name="norm_matmul",
    )(x, g, w)


def _conv_post(y, b, lg, lb):
    y = y + b
    mu = jnp.mean(y, axis=-1, keepdims=True)
    yc = y - mu
    var = jnp.mean(yc * yc, axis=-1, keepdims=True)
    return _silu(yc * lax.rsqrt(var + LN_EPS) * lg + lb)


def _conv_prompt_kernel(av_ref, ag_ref, w_ref, b_ref, lg_ref, lb_ref, y_ref, nb_ref, ubuf, *, tt, rb):
    t = pl.program_id(1)

    @pl.when(t == 0)
    def _():
        ubuf[0:CONV_HALO, :] = jnp.zeros((CONV_HALO, D_A), F32)

    @pl.when(t > 0)
    def _():
        ubuf[0:CONV_HALO, :] = ubuf[tt:tt + CONV_HALO, :]

    ubuf[CONV_HALO:CONV_HALO + tt, :] = av_ref[0].astype(F32) * _sigmoid(ag_ref[0].astype(F32))
    off = CONV_HALO - (CONV_W - 1)
    span = rb + CONV_HALO
    for r in range(tt // rb):
        win = ubuf[r * rb:r * rb + span, :]
        acc = jnp.zeros((rb, D_A), F32)
        for p in range(8):
            wp = win if p == 0 else pltpu.roll(win, span - p, 0)
            for q in range(CONV_HALO // 8 + 1):
                k = 8 * q + p - off
                if 0 <= k < CONV_W:
                    acc = acc + w_ref[k:k + 1, :] * wp[8 * q:8 * q + rb, :]
        y_ref[0, r * rb:(r + 1) * rb, :] = _conv_post(acc, b_ref[...], lg_ref[...], lb_ref[...]).astype(y_ref.dtype)

    @pl.when(t == pl.num_programs(1) - 1)
    def _():
        nb_ref[0] = ubuf[tt + off:tt + CONV_HALO, :]


def _conv_prompt(proj3, w, b, lg, lb):
    bsz, t, _ = proj3.shape
    tt = min(256, t)
    kern = functools.partial(_conv_prompt_kernel, tt=tt, rb=32)
    vec = pl.BlockSpec((1, D_A), lambda i, j: (0, 0))
    return pl.pallas_call(
        kern,
        grid=(bsz, t // tt),
        in_specs=[
            pl.BlockSpec((1, tt, D_A), lambda i, j: (i, j, 0)),
            pl.BlockSpec((1, tt, D_A), lambda i, j: (i, j, 1)),
            pl.BlockSpec((CONV_W, D_A), lambda i, j: (0, 0)),
            vec, vec, vec,
        ],
        out_specs=[
            pl.BlockSpec((1, tt, D_A), lambda i, j: (i, j, 0)),
            pl.BlockSpec((1, CONV_W - 1, D_A), lambda i, j: (i, 0, 0)),
        ],
        out_shape=[
            jax.ShapeDtypeStruct((bsz, t, D_A), BF16),
            jax.ShapeDtypeStruct((bsz, CONV_W - 1, D_A), F32),
        ],
        scratch_shapes=[pltpu.VMEM((CONV_HALO + tt, D_A), F32)],
        compiler_params=_cparams("parallel", "arbitrary"),
        name="conv_prompt",
    )(proj3, proj3, w, b, lg, lb)


def _conv_sample_kernel(av_ref, ag_ref, buf_ref, w_ref, b_ref, lg_ref, lb_ref, y_ref, nb_ref):
    u = av_ref[...] * _sigmoid(ag_ref[...])
    acc = w_ref[CONV_W - 1:CONV_W, :] * u
    for k in range(CONV_W - 1):
        acc = acc + w_ref[k:k + 1, :] * buf_ref[k]
    y_ref[...] = _conv_post(acc, b_ref[...], lg_ref[...], lb_ref[...]).astype(y_ref.dtype)
    for k in range(CONV_W - 2):
        nb_ref[k] = buf_ref[k + 1]
    nb_ref[CONV_W - 2] = u


def _conv_sample(proj, buf_t, w, b, lg, lb):
    n = proj.shape[0]
    nbk = min(32, n)
    vec = pl.BlockSpec((1, D_A), lambda i: (0, 0))
    return pl.pallas_call(
        _conv_sample_kernel,
        grid=(n // nbk,),
        in_specs=[
            pl.BlockSpec((nbk, D_A), lambda i: (i, 0)),
            pl.BlockSpec((nbk, D_A), lambda i: (i, 1)),
            pl.BlockSpec((CONV_W - 1, nbk, D_A), lambda i: (0, i, 0)),
            pl.BlockSpec((CONV_W, D_A), lambda i: (0, 0)),
            vec, vec, vec,
        ],
        out_specs=[
            pl.BlockSpec((nbk, D_A), lambda i: (i, 0)),
            pl.BlockSpec((CONV_W - 1, nbk, D_A), lambda i: (0, i, 0)),
        ],
        out_shape=[
            jax.ShapeDtypeStruct((n, D_A), F32),
            jax.ShapeDtypeStruct((CONV_W - 1, n, D_A), F32),
        ],
        compiler_params=_cparams("parallel"),
        name="conv_sample",
    )(proj, proj, buf_t, w, b, lg, lb)


def _s5_kernel(u_ref, h0r_ref, h0i_ref, ar_ref, ai_ref, bcat_ref, ccat_ref, d_ref, gw_ref, gb_ref,
               y_ref, hr_out, hi_out, sre, sim, cre, cim, rt, *, tt, nb, lw, seq_major):
    i = pl.program_id(0)
    m = tt * nb
    cw = S5_STATE // S5_CHUNKS

    @pl.when(i == 0)
    def _():
        cre[...] = h0r_ref[...]
        cim[...] = h0i_ref[...]

    if seq_major:
        for b in range(nb):
            rt[:, b, :] = u_ref[b].astype(F32)
        u = rt[...].reshape(m, D_B)
    else:
        u = u_ref[...].reshape(m, D_B)
    ub = u.astype(BF16)
    for c in range(S5_CHUNKS):
        bu = jnp.dot(ub[:, c * LANES:(c + 1) * LANES], bcat_ref[c], preferred_element_type=F32)
        sre[:, :, c * cw:(c + 1) * cw] = bu[:, :cw].reshape(tt, nb, cw)
        sim[:, :, c * cw:(c + 1) * cw] = bu[:, cw:].reshape(tt, nb, cw)

    for c in range(S5_STATE // lw):
        ls = slice(c * lw, (c + 1) * lw)
        ar = jnp.broadcast_to(ar_ref[:, ls], (nb, lw))
        ai = jnp.broadcast_to(ai_ref[:, ls], (nb, lw))

        def body(t, carry, ls=ls, ar=ar, ai=ai):
            hr, hi = carry
            nr = ar * hr - ai * hi + sre[t, :, ls]
            ni = ar * hi + ai * hr + sim[t, :, ls]
            sre[t, :, ls] = nr
            sim[t, :, ls] = ni
            return nr, ni

        hr, hi = lax.fori_loop(0, tt, body, (cre[:, ls], cim[:, ls]), unroll=min(tt, 8))
        cre[:, ls] = hr
        cim[:, ls] = hi

    hre = sre[...].reshape(m, S5_STATE)
    him = sim[...].reshape(m, S5_STATE)
    ys = []
    for c in range(S5_CHUNKS):
        hcat = jnp.concatenate([hre[:, c * cw:(c + 1) * cw], him[:, c * cw:(c + 1) * cw]], axis=1)
        ys.append(jnp.dot(hcat.astype(BF16), ccat_ref[c], preferred_element_type=F32))
    y = jnp.concatenate(ys, axis=1) + d_ref[...] * u
    y = 0.5 * y * (1.0 + lax.erf(y * (1.0 / math.sqrt(2.0))))
    z = jnp.dot(y.astype(BF16), gw_ref[...], preferred_element_type=F32) + gb_ref[...]
    out = (y * _sigmoid(z)).reshape(tt, nb, D_B)
    if seq_major:
        rt[...] = out
        for b in range(nb):
            y_ref[b] = rt[:, b, :].astype(y_ref.dtype)
    else:
        y_ref[...] = out.astype(y_ref.dtype)

    @pl.when(i == pl.num_programs(0) - 1)
    def _():
        hr_out[...] = cre[...]
        hi_out[...] = cim[...]


def _s5(u, h0r, h0i, abar_re, abar_im, bcat, ccat, d, glu_w, glu_b, tt, seq_major):
    if seq_major:
        nb, t, c = u.shape
        u_spec = pl.BlockSpec((nb, tt, D_B), lambda i: (0, i, c // D_B - 1))
        y_spec = pl.BlockSpec((nb, tt, D_B), lambda i: (0, i, 0))
        y_shape = (nb, t, D_B)
    else:
        t, nb, _ = u.shape
        u_spec = pl.BlockSpec((tt, nb, D_B), lambda i: (i, 0, 0))
        y_spec = u_spec
        y_shape = (t, nb, D_B)
    lw = max(LANES, min(512, 8 * 1024 // nb))
    kern = functools.partial(_s5_kernel, tt=tt, nb=nb, lw=lw, seq_major=seq_major)
    full = lambda shape: pl.BlockSpec(shape, lambda i: (0,) * len(shape))
    return pl.pallas_call(
        kern,
        grid=(t // tt,),
        in_specs=[
            u_spec,
            full((nb, S5_STATE)), full((nb, S5_STATE)),
            full((1, S5_STATE)), full((1, S5_STATE)),
            full(bcat.shape), full(ccat.shape),
            full((1, D_B)), full((D_B, D_B)), full((1, D_B)),
        ],
        out_specs=[
            y_spec,
            full((nb, S5_STATE)), full((nb, S5_STATE)),
        ],
        out_shape=[
            jax.ShapeDtypeStruct(y_shape, BF16 if seq_major else F32),
            jax.ShapeDtypeStruct((nb, S5_STATE), F32),
            jax.ShapeDtypeStruct((nb, S5_STATE), F32),
        ],
        scratch_shapes=[
            pltpu.VMEM((tt, nb, S5_STATE), F32), pltpu.VMEM((tt, nb, S5_STATE), F32),
            pltpu.VMEM((nb, S5_STATE), F32), pltpu.VMEM((nb, S5_STATE), F32),
            pltpu.VMEM((tt, nb, D_B), F32),
        ],
        compiler_params=_cparams("arbitrary"),
        name="s5",
    )(u, h0r, h0i, abar_re, abar_im, bcat, ccat, d, glu_w, glu_b)


def _s5_params(a_re, a_im, log_dt, b_re, b_im, c_re, c_im):
    dt = jnp.exp(log_dt)[:, None]
    mag = jnp.exp(dt * a_re)
    ang = dt * a_im
    abar_re, abar_im = mag * jnp.cos(ang), mag * jnp.sin(ang)
    den = a_re * a_re + a_im * a_im
    nr, ni = abar_re - 1.0, abar_im
    coef_re = (nr * a_re + ni * a_im) / den
    coef_im = (ni * a_re - nr * a_im) / den
    bbar_re = coef_re[..., None] * b_re - coef_im[..., None] * b_im
    bbar_im = coef_re[..., None] * b_im + coef_im[..., None] * b_re
    gpc = S5_G // S5_CHUNKS
    eye = jnp.eye(gpc, dtype=F32)

    def bblk(x):
        x = x.reshape(S5_CHUNKS, gpc, S5_P, S5_GH)
        return jnp.einsum("cgph,gk->cghkp", x, eye).reshape(S5_CHUNKS, gpc * S5_GH, gpc * S5_P)

    def cblk(x):
        x = x.reshape(S5_CHUNKS, gpc, S5_GH, S5_P)
        return jnp.einsum("cghp,gk->cgpkh", x, eye).reshape(S5_CHUNKS, gpc * S5_P, gpc * S5_GH)

    bcat = jnp.concatenate([bblk(bbar_re), bblk(bbar_im)], axis=2).astype(BF16)
    ccat = jnp.concatenate([cblk(c_re), -cblk(c_im)], axis=1).astype(BF16)
    return abar_re.reshape(1, S5_STATE), abar_im.reshape(1, S5_STATE), bcat, ccat


def _hg_gates(q, f, lb):
    qf = _silu(q)
    sig = _sigmoid(f)
    forget = lb + (1.0 - lb) * sig
    logg = jnp.log(jnp.maximum(forget, FORGET_FLOOR))
    kf = (1.0 - lb) * (1.0 - sig)
    return qf, logg, kf


def _hg_out(o, g, gn):
    return o * lax.rsqrt(jnp.mean(o * o, axis=-1, keepdims=True) + EPS) * gn * _silu(g)


def _cumsum_rows(x, tri):
    hi = x.astype(BF16)
    r1 = x - hi.astype(F32)
    mid = r1.astype(BF16)
    lo = (r1 - mid.astype(F32)).astype(BF16)
    dot = lambda p: jnp.dot(tri, p, preferred_element_type=F32)
    return dot(hi) + dot(mid) + dot(lo)


def _hgrn_prompt_kernel(xc_ref, xn_ref, nm_ref, w_ref, lb_ref, gn_ref, o_ref, s_out, st, obuf, pbuf, *, tt):
    t = pl.program_id(1)
    L = HG_CHUNK
    dc = HG_HEADS * HG_K
    step = pl.program_id(0) * pl.num_programs(1) + t
    slot = step % 2

    def project(x_ref):
        xn = _rms(x_ref[0], nm_ref[...]).astype(BF16)
        return jnp.dot(xn, w_ref[...], preferred_element_type=F32).astype(BF16)

    def project_next():
        pbuf[1 - slot] = project(xn_ref)

    @pl.when(step == 0)
    def _():
        pbuf[0] = project(xc_ref)

    @pl.when(t == 0)
    def _():
        st[...] = jnp.zeros(st.shape, F32)

    q_of = lambda rs: pbuf[slot, rs, 0:dc]
    f_of = lambda rs: pbuf[slot, rs, dc:2 * dc]
    v_of = lambda rs: pbuf[slot, rs, 2 * dc:3 * dc]
    g_of = lambda rs, hs: pbuf[slot, rs, 3 * dc + hs.start:3 * dc + hs.stop]

    row = lax.broadcasted_iota(I32, (L, L), 0)
    col = lax.broadcasted_iota(I32, (L, L), 1)
    causal = row >= col
    tri = causal.astype(BF16)
    nt = (((1,), (1,)), ((), ()))
    tn = (((0,), (0,)), ((), ()))
    chunks = []
    spread = None
    for c in range(tt // L):
        rs = slice(c * L, (c + 1) * L)
        q, logg, k = _hg_gates(q_of(rs).astype(F32), f_of(rs).astype(F32), lb_ref[...])
        vb = v_of(rs)
        gc = _cumsum_rows(logg, tri)
        gmid = gc[L // 2 - 1:L // 2, :]
        glast = gc[L - 1:L, :]
        sp = jnp.maximum(gc[0:1, :] - gmid, gmid - glast)
        spread = sp if spread is None else jnp.maximum(spread, sp)
        chunks.append((rs, q, logg, k, vb, gc, gmid, glast))
    fast = jnp.max(spread) <= EXP_RANGE

    @pl.when(fast)
    def _():
        for rs, q, logg, k, vb, gc, gmid, glast in chunks:
            qe = (q * jnp.exp(gc - gmid)).astype(BF16)
            ke = (k * jnp.exp(gmid - gc)).astype(BF16)
            qg = (q * jnp.exp(gc)).astype(BF16)
            kd = (k * jnp.exp(glast - gc)).astype(BF16)
            dlast = jnp.exp(glast)
            for h in range(HG_HEADS):
                hs = slice(h * HG_K, (h + 1) * HG_K)
                sc = lax.dot_general(qe[:, hs], ke[:, hs], nt, preferred_element_type=F32)
                sc = jnp.where(causal, sc, 0.0).astype(BF16)
                s_t = st[h]
                o = (jnp.dot(sc, vb[:, hs], preferred_element_type=F32)
                     + lax.dot_general(qg[:, hs], s_t.astype(BF16), nt, preferred_element_type=F32))
                st[h] = dlast[:, hs] * s_t + lax.dot_general(vb[:, hs], kd[:, hs], tn,
                                                             preferred_element_type=F32)
                o_ref[0, rs, hs] = _hg_out(o, g_of(rs, hs).astype(F32), gn_ref[...]).astype(o_ref.dtype)
        project_next()

    @pl.when(jnp.logical_not(fast))
    def _():
        for rs, q, logg, k, vb, gc, gmid, glast in chunks:
            dec = jnp.exp(logg)
            kb = k.astype(BF16)
            rsel = lax.broadcasted_iota(I32, (L, 1), 0)
            obuf[...] = jnp.zeros(obuf.shape, F32)

            def step(t, carry):
                sel = rsel == t
                g_t = jnp.sum(jnp.where(sel, dec, 0.0), axis=0, keepdims=True)
                v_t = jnp.where(sel, vb, jnp.zeros_like(vb))
                q_t = jnp.where(sel, q, 0.0).astype(BF16)
                for h in range(HG_HEADS):
                    hs = slice(h * HG_K, (h + 1) * HG_K)
                    s_new = g_t[:, hs] * st[h] + lax.dot_general(v_t[:, hs], kb[:, hs], tn,
                                                                 preferred_element_type=F32)
                    st[h] = s_new
                    obuf[:, hs] += lax.dot_general(q_t[:, hs], s_new.astype(BF16), nt,
                                                   preferred_element_type=F32)
                return carry

            lax.fori_loop(0, L, step, 0)
            for h in range(HG_HEADS):
                hs = slice(h * HG_K, (h + 1) * HG_K)
                o_ref[0, rs, hs] = _hg_out(obuf[:, hs], g_of(rs, hs).astype(F32),
                                           gn_ref[...]).astype(o_ref.dtype)
        project_next()

    @pl.when(t == pl.num_programs(1) - 1)
    def _():
        for h in range(HG_HEADS):
            s_out[0, h] = st[h].T


def _hgrn_prompt(h3, g_mix, w_in, lb, gn):
    bsz, t, d = h3.shape
    tt = min(256, t)
    nt = t // tt
    dc = HG_HEADS * HG_K
    kern = functools.partial(_hgrn_prompt_kernel, tt=tt)

    def nxt(i, j):
        f = jnp.minimum(i * nt + j + 1, bsz * nt - 1)
        return (f // nt, f % nt, 0)

    return pl.pallas_call(
        kern,
        grid=(bsz, nt),
        in_specs=[pl.BlockSpec((1, tt, d), lambda i, j: (i, j, 0)),
                  pl.BlockSpec((1, tt, d), nxt),
                  pl.BlockSpec((1, d), lambda i, j: (0, 0)),
                  pl.BlockSpec((d, 4 * dc), lambda i, j: (0, 0)),
                  pl.BlockSpec((1, dc), lambda i, j: (0, 0)),
                  pl.BlockSpec((1, HG_V), lambda i, j: (0, 0))],
        out_specs=[
            pl.BlockSpec((1, tt, dc), lambda i, j: (i, j, 0)),
            pl.BlockSpec((1, HG_HEADS, HG_K, HG_V), lambda i, j: (i, 0, 0, 0)),
        ],
        out_shape=[
            jax.ShapeDtypeStruct((bsz, t, dc), BF16),
            jax.ShapeDtypeStruct((bsz, HG_HEADS, HG_K, HG_V), F32),
        ],
        scratch_shapes=[pltpu.VMEM((HG_HEADS, HG_V, HG_K), F32), pltpu.VMEM((HG_CHUNK, dc), F32),
                        pltpu.VMEM((2, tt, 4 * dc), BF16)],
        compiler_params=_cparams("arbitrary", "arbitrary"),
        name="hgrn_prompt",
    )(h3, h3, g_mix, w_in, lb, gn)


def _hgrn_sample_kernel(q_ref, f_ref, v_ref, g_ref, s_ref, lb_ref, gn_ref, o_ref, s_out, *, nbk):
    for h in range(HG_HEADS):
        hs = slice(h * HG_K, (h + 1) * HG_K)
        q, logg, k = _hg_gates(q_ref[:, hs], f_ref[:, hs], lb_ref[:, hs])
        dec = jnp.exp(logg)
        v = v_ref[:, hs]
        rows = []
        for n in range(nbk):
            col = lambda x: jnp.broadcast_to(x[n:n + 1, :], (HG_K, HG_K)).T
            s_new = col(dec) * s_ref[0, n, h] + col(k) * v[n:n + 1, :]
            s_out[n, h] = s_new
            rows.append(jnp.sum(col(q) * s_new, axis=0, keepdims=True))
        o = jnp.concatenate(rows, axis=0)
        o_ref[:, hs] = _hg_out(o, g_ref[:, hs], gn_ref[...]).astype(o_ref.dtype)


def _hgrn_sample(proj, s_all, layer, lb, gn):
    n = proj.shape[0]
    nbk = 8
    dc = HG_HEADS * HG_K
    kern = functools.partial(_hgrn_sample_kernel, nbk=nbk)
    blk = lambda c: pl.BlockSpec((nbk, dc), lambda i, c=c: (i, c))
    return pl.pallas_call(
        kern,
        grid=(n // nbk,),
        in_specs=[blk(0), blk(1), blk(2), blk(3),
                  pl.BlockSpec((1, nbk, HG_HEADS, HG_K, HG_V), lambda i: (layer, i, 0, 0, 0)),
                  pl.BlockSpec((1, dc), lambda i: (0, 0)),
                  pl.BlockSpec((1, HG_V), lambda i: (0, 0))],
        out_specs=[pl.BlockSpec((nbk, dc), lambda i: (i, 0)),
                   pl.BlockSpec((nbk, HG_HEADS, HG_K, HG_V), lambda i: (i, 0, 0, 0))],
        out_shape=[
            jax.ShapeDtypeStruct((n, dc), F32),
            jax.ShapeDtypeStruct(s_all.shape[1:], F32),
        ],
        compiler_params=_cparams("parallel"),
        name="hgrn_sample",
    )(proj, proj, proj, proj, s_all, lb, gn)


def _mix_ffn_kernel(h_ref, ya_ref, yb_ref, woa_ref, wob_ref, g_ref, wg_ref, wu_ref, wd_ref, o_ref,
                    h1_ref, xn_ref, acc_ref, *, precise):
    k = pl.program_id(1)

    @pl.when(k == 0)
    def _():
        h1 = (h_ref[...] + _pdot(ya_ref[...], woa_ref[...], precise)
              + _pdot(yb_ref[...], wob_ref[...], precise))
        h1_ref[...] = h1
        xn_ref[...] = _rms(h1, g_ref[...]).astype(xn_ref.dtype)
        acc_ref[...] = jnp.zeros(acc_ref.shape, F32)

    xn = xn_ref[...]
    a = _pdot(xn, wg_ref[...], precise)
    b = _pdot(xn, wu_ref[...], precise)
    acc_ref[...] += _pdot(_silu(a) * b, wd_ref[...], precise)

    @pl.when(k == pl.num_programs(1) - 1)
    def _():
        o_ref[...] = h1_ref[...] + acc_ref[...]


def _mix_ffn(h, ya, yb, woa, wob, g, wg, wu, wd):
    precise = wg.dtype == F32
    n, d = h.shape
    tm = min(512, n)
    tf = D_FF_PAD // 2
    row = lambda w: pl.BlockSpec((tm, w), lambda i, k: (i, 0))
    return pl.pallas_call(
        functools.partial(_mix_ffn_kernel, precise=precise),
        grid=(n // tm, D_FF_PAD // tf),
        in_specs=[
            row(d), row(D_A), row(D_B),
            pl.BlockSpec((D_A, d), lambda i, k: (0, 0)),
            pl.BlockSpec((D_B, d), lambda i, k: (0, 0)),
            pl.BlockSpec((1, d), lambda i, k: (0, 0)),
            pl.BlockSpec((d, tf), lambda i, k: (0, k)),
            pl.BlockSpec((d, tf), lambda i, k: (0, k)),
            pl.BlockSpec((tf, d), lambda i, k: (k, 0)),
        ],
        out_specs=row(d),
        out_shape=jax.ShapeDtypeStruct((n, d), F32),
        scratch_shapes=[pltpu.VMEM((tm, d), F32), pltpu.VMEM((tm, d), F32 if precise else BF16),
                        pltpu.VMEM((tm, d), F32)],
        compiler_params=_cparams("parallel", "arbitrary"),
        name="mix_ffn",
    )(h, ya, yb, woa, wob, g, wg, wu, wd)


def _mix_router_kernel(h_ref, y_ref, wo_ref, g_ref, rh_ref, rl_ref, *rest, precise):
    h1_ref, xn_ref, ids_ref, gts_ref = rest[-4:]
    h1 = h_ref[...] + _pdot(y_ref[...], wo_ref[...], precise)
    h1_ref[...] = h1
    xn = _rms(h1, g_ref[...])
    xn_ref[...] = xn
    xh = xn.astype(BF16)
    xl = (xn - xh.astype(F32)).astype(BF16)
    dot = lambda a, b: jnp.dot(a, b, preferred_element_type=F32)
    logits = dot(xh, rh_ref[...]) + (dot(xh, rl_ref[...]) + dot(xl, rh_ref[...]))
    lane = lax.broadcasted_iota(I32, logits.shape, 1)
    neg = jnp.float32(-jnp.inf)
    logits = jnp.where(lane < N_EXP, logits, neg)
    m1 = jnp.max(logits, axis=-1, keepdims=True)
    i1 = jnp.min(jnp.where(logits == m1, lane, LANES), axis=-1, keepdims=True)
    rest = jnp.where(lane == i1, neg, logits)
    m2 = jnp.max(rest, axis=-1, keepdims=True)
    i2 = jnp.min(jnp.where(rest == m2, lane, LANES), axis=-1, keepdims=True)
    e2 = jnp.exp(m2 - m1)
    g1 = 1.0 / (1.0 + e2)
    g2 = e2 / (1.0 + e2)
    ids = jnp.where(lane == 0, i1, jnp.where(lane == 1, i2, 0))
    ids_ref[...] = ids.T[0:8, :]
    gts_ref[...] = jnp.where(lane == 0, g1, jnp.where(lane == 1, g2, 0.0))


def _mix_router(h, y, wo, g, r_hi, r_lo, xn_all, row0):
    n, d = h.shape
    tm = min(1024, n)
    row = lambda w: pl.BlockSpec((tm, w), lambda i: (i, 0))
    return pl.pallas_call(
        functools.partial(_mix_router_kernel, precise=wo.dtype == F32),
        grid=(n // tm,),
        in_specs=[row(d), row(d),
                  pl.BlockSpec((d, d), lambda i: (0, 0)),
                  pl.BlockSpec((1, d), lambda i: (0, 0)),
                  pl.BlockSpec((d, LANES), lambda i: (0, 0)),
                  pl.BlockSpec((d, LANES), lambda i: (0, 0)),
                  pl.BlockSpec(memory_space=pl.ANY)],
        out_specs=[row(d), pl.BlockSpec((tm, d), lambda i: (i + row0 // tm, 0)),
                   pl.BlockSpec((8, tm), lambda i: (0, i)), row(LANES)],
        out_shape=[
            jax.ShapeDtypeStruct((n, d), F32),
            jax.ShapeDtypeStruct(xn_all.shape, F32),
            jax.ShapeDtypeStruct((8, n), I32),
            jax.ShapeDtypeStruct((n, LANES), F32),
        ],
        input_output_aliases={6: 1},
        compiler_params=_cparams("parallel"),
        name="mix_router",
    )(h, y, wo, g, r_hi, r_lo, xn_all)


def _wait_rows(src, dst, sem, copies):
    for _ in range(copies):
        pltpu.make_async_copy(src, dst, sem).wait()


def _dispatch_kernel(pos_ref, x_ref, xs_in, xs_hbm, sem, *, tt, n_tok):
    del xs_in
    i = pl.program_id(0)

    def body(r, c):
        t = i * tt + r
        for j in range(2):
            pltpu.make_async_copy(x_ref.at[pl.ds(r, 1), :], xs_hbm.at[pl.ds(pos_ref[j * n_tok + t], 1), :],
                                  sem.at[0]).start()
        return c

    lax.fori_loop(0, tt, body, 0, unroll=8)
    _wait_rows(x_ref, xs_hbm.at[pl.ds(0, tt), :], sem.at[0], 2)


def _dispatch(x_all, pos, xs_zero):
    n, d = x_all.shape
    tt = 384 if n % 384 == 0 else LANES
    kern = functools.partial(_dispatch_kernel, tt=tt, n_tok=n)
    grid_spec = pltpu.PrefetchScalarGridSpec(
        num_scalar_prefetch=1,
        grid=(n // tt,),
        in_specs=[pl.BlockSpec((tt, d), lambda i, p: (i, 0)), pl.BlockSpec(memory_space=pl.ANY)],
        out_specs=pl.BlockSpec(memory_space=pl.ANY),
        scratch_shapes=[pltpu.SemaphoreType.DMA((1,))],
    )
    return pl.pallas_call(
        kern,
        grid_spec=grid_spec,
        out_shape=jax.ShapeDtypeStruct(xs_zero.shape, F32),
        input_output_aliases={2: 0},
        compiler_params=_cparams("arbitrary"),
        name="moe_dispatch",
    )(pos, x_all, xs_zero)


def _expert_kernel(te_ref, nu_ref, xs_ref, wg_ref, wu_ref, wd_ref, ys_ref, xb, acc_ref):
    i = pl.program_id(0)
    k = pl.program_id(1)
    used = i < nu_ref[0]

    @pl.when(k == 0)
    def _():
        acc_ref[...] = jnp.zeros(acc_ref.shape, F32)
        xb[...] = xs_ref[...].astype(BF16)

    @pl.when(used)
    def _():
        x = xb[...]
        a = jnp.dot(x, wg_ref[0, 0].astype(BF16), preferred_element_type=F32)
        b = jnp.dot(x, wu_ref[0, 0].astype(BF16), preferred_element_type=F32)
        acc_ref[...] += jnp.dot((_silu(a) * b).astype(BF16), wd_ref[0, 0].astype(BF16),
                                preferred_element_type=F32)

    @pl.when(k == pl.num_programs(1) - 1)
    def _():
        ys_ref[...] = acc_ref[...]


def _experts(xs, tile_expert, n_used, wg, wu, wd, layer):
    n_rows = xs.shape[0]
    tm = MOE_TM
    nk = 7
    tk = D_EXP // nk

    def kk(i, k, nu):
        return jnp.where(i < nu[0], k, nk - 1)

    grid_spec = pltpu.PrefetchScalarGridSpec(
        num_scalar_prefetch=2,
        grid=(n_rows // tm, nk),
        in_specs=[
            pl.BlockSpec((tm, D_MODEL), lambda i, k, te, nu: (i, 0)),
            pl.BlockSpec((1, 1, D_MODEL, tk), lambda i, k, te, nu: (layer, te[i], 0, kk(i, k, nu))),
            pl.BlockSpec((1, 1, D_MODEL, tk), lambda i, k, te, nu: (layer, te[i], 0, kk(i, k, nu))),
            pl.BlockSpec((1, 1, tk, D_MODEL), lambda i, k, te, nu: (layer, te[i], kk(i, k, nu), 0)),
        ],
        out_specs=pl.BlockSpec((tm, D_MODEL), lambda i, k, te, nu: (i, 0)),
        scratch_shapes=[pltpu.VMEM((tm, D_MODEL), BF16), pltpu.VMEM((tm, D_MODEL), F32)],
    )
    return pl.pallas_call(
        _expert_kernel,
        grid_spec=grid_spec,
        out_shape=jax.ShapeDtypeStruct((n_rows, D_MODEL), F32),
        compiler_params=_cparams("arbitrary", "arbitrary"),
        name="moe_experts",
    )(tile_expert, n_used, xs, wg, wu, wd)


def _combine_kernel(pos_ref, h_ref, g_ref, ys_hbm, fin_ref, o_ref, buf, sem, *, tt, tok0, n_tok, final):
    i = pl.program_id(0)

    def issue(tile, slot):
        def body(r, c):
            t = tok0 + tile * tt + r
            for j in range(2):
                pltpu.make_async_copy(ys_hbm.at[pl.ds(pos_ref[j * n_tok + t], 1), :],
                                      buf.at[slot, j, pl.ds(r, 1), :], sem.at[slot]).start()
            return c

        lax.fori_loop(0, tt, body, 0, unroll=8)

    @pl.when(i == 0)
    def _():
        issue(0, 0)

    @pl.when(i + 1 < pl.num_programs(0))
    def _():
        issue(i + 1, (i + 1) % 2)

    slot = i % 2
    _wait_rows(ys_hbm.at[pl.ds(0, tt), :], buf.at[slot, 0], sem.at[slot], 2)
    g = g_ref[...]
    out = h_ref[...] + (g[:, 0:1] * buf[slot, 0] + g[:, 1:2] * buf[slot, 1])
    o_ref[...] = _rms(out, fin_ref[...]) if final else out


def _combine(h1, gts, pos, ys, tok0, fin_g, final):
    n, d = h1.shape
    tt = min(512, n)
    kern = functools.partial(_combine_kernel, tt=tt, tok0=tok0, n_tok=pos.shape[0] // 2, final=final)
    grid_spec = pltpu.PrefetchScalarGridSpec(
        num_scalar_prefetch=1,
        grid=(n // tt,),
        in_specs=[pl.BlockSpec((tt, d), lambda i, p: (i, 0)),
                  pl.BlockSpec((tt, LANES), lambda i, p: (i, 0)),
                  pl.BlockSpec(memory_space=pl.ANY),
                  pl.BlockSpec((1, d), lambda i, p: (0, 0))],
        out_specs=pl.BlockSpec((tt, d), lambda i, p: (i, 0)),
        scratch_shapes=[pltpu.VMEM((2, 2, tt, d), F32), pltpu.SemaphoreType.DMA((2,))],
    )
    return pl.pallas_call(
        kern,
        grid_spec=grid_spec,
        out_shape=jax.ShapeDtypeStruct((n, d), F32),
        compiler_params=_cparams("arbitrary"),
        name="moe_combine",
    )(pos, h1, gts, ys, fin_g)


def _moe(h1_p, h1_s, x_all, ids_p, ids_s, gts_p, gts_s, xs_zero, wg, wu, wd, layer, fin_g, final):
    n_p = h1_p.shape[0]
    tm = MOE_TM
    n_tiles = xs_zero.shape[0] // tm
    e = jnp.concatenate([ids_p[:2], ids_s[:2]], axis=1)
    experts = jnp.arange(N_EXP, dtype=I32)[:, None]
    is_e = [e[j][None, :] == experts for j in range(2)]
    hit = is_e[0].astype(I32) + is_e[1].astype(I32)
    cum = jnp.cumsum(hit, axis=1)
    rank = cum - hit
    counts = cum[:, -1]
    padded = ((counts + tm - 1) // tm) * tm
    ends = jnp.cumsum(padded)
    offs = ends - padded
    row_of = offs[:, None] + rank
    pos = jnp.stack([jnp.sum(jnp.where(m, row_of, 0), axis=0) for m in is_e]).reshape(-1).astype(I32)
    n_used = (ends[-1] // tm).astype(I32).reshape(1)
    tile_start = jnp.arange(n_tiles, dtype=I32) * tm
    tile_expert = jnp.sum((ends[None, :] <= tile_start[:, None]).astype(I32), axis=1)
    tile_expert = jnp.minimum(tile_expert, N_EXP - 1).astype(I32)

    xs = _dispatch(x_all, pos, xs_zero)
    ys = _experts(xs, tile_expert, n_used, wg, wu, wd, layer)
    out_p = _combine(h1_p, gts_p, pos, ys, 0, fin_g, final)
    out_s = _combine(h1_s, gts_s, pos, ys, n_p, fin_g, final)
    return out_p, out_s


def _final_norm_kernel(x_ref, g_ref, o_ref):
    o_ref[...] = _rms(x_ref[...], g_ref[...])


def _final_norm(x, g):
    n, d = x.shape
    tm = min(1024, n)
    return pl.pallas_call(
        _final_norm_kernel,
        grid=(n // tm,),
        in_specs=[pl.BlockSpec((tm, d), lambda i: (i, 0)), pl.BlockSpec((1, d), lambda i: (0, 0))],
        out_specs=pl.BlockSpec((tm, d), lambda i: (i, 0)),
        out_shape=jax.ShapeDtypeStruct((n, d), F32),
        compiler_params=_cparams("parallel"),
        name="final_norm",
    )(x, g)


def kernel(x_prompt, x_sample, state_conv, state_ssm_re, state_ssm_im, state_hgrn, norm_mix, norm_ffn, norm_final, ab_w_in, ab_w_out, conv_w, conv_b, conv_ln_g, conv_ln_b, ssm_a_re, ssm_a_im, ssm_log_dt, ssm_b_re, ssm_b_im, ssm_c_re, ssm_c_im, ssm_d, ssm_glu_w, ssm_glu_b, hg_w_in, hg_lb_logits, hg_gnorm, hg_w_out, ffn_w_gate, ffn_w_up, ffn_w_down, moe_router, moe_w_gate, moe_w_up, moe_w_down):
    bsz, seq, d = x_prompt.shape
    n_s = x_sample.shape[0]
    depth = norm_mix.shape[0]
    hp = x_prompt.reshape(bsz * seq, d)
    hs = x_sample.reshape(n_s, d)

    lbs = jax.nn.softmax(hg_lb_logits.astype(F32), axis=0)
    lbs = jnp.cumsum(lbs, axis=0) - lbs[0:1]

    conv_p, re_p, im_p, hg_p = [], [], [], []
    conv_s, re_s, im_s, hg_s = [], [], [], []
    zero_state = jnp.zeros((bsz, S5_STATE), F32)
    x_all = jnp.zeros((bsz * seq + n_s, d), F32)
    moe_wg, moe_wu, moe_wd = moe_w_gate.astype(F32), moe_w_up.astype(F32), moe_w_down.astype(F32)
    row = lambda v: v.reshape(1, -1).astype(F32)
    g_fin = row(norm_final)

    for l in range(depth):
        j = l // 2
        g_mix, g_ffn = row(norm_mix[l]), row(norm_ffn[l])
        if l % 2 == 0:
            w_in = ab_w_in[j].astype(BF16)
            woa = ab_w_out[j, :D_A].astype(BF16)
            wob = ab_w_out[j, D_A:].astype(BF16)
            cw, cb, clg, clb = conv_w[j].astype(F32), row(conv_b[j]), row(conv_ln_g[j]), row(conv_ln_b[j])
            abar_re, abar_im, bcat, ccat = _s5_params(
                ssm_a_re[j].astype(F32), ssm_a_im[j].astype(F32), ssm_log_dt[j].astype(F32),
                ssm_b_re[j].astype(F32), ssm_b_im[j].astype(F32), ssm_c_re[j].astype(F32), ssm_c_im[j].astype(F32))
            sd, sgw, sgb = row(ssm_d[j]), ssm_glu_w[j].astype(BF16), row(ssm_glu_b[j])
            pad = ((0, 0), (0, D_FF_PAD - D_FF))
            wg = jnp.pad(ffn_w_gate[j], pad).astype(BF16)
            wu = jnp.pad(ffn_w_up[j], pad).astype(BF16)
            wd = jnp.pad(ffn_w_down[j], (pad[1], pad[0])).astype(BF16)

            proj = _norm_matmul(hp, g_mix, w_in, tn=w_in.shape[1], out_dtype=BF16)
            proj3 = proj.reshape(bsz, seq, -1)
            ya, cb_new = _conv_prompt(proj3, cw, cb, clg, clb)
            yb, hr, hi = _s5(proj3, zero_state, zero_state, abar_re, abar_im, bcat, ccat, sd, sgw, sgb,
                             tt=min(128, seq), seq_major=True)
            hp = _mix_ffn(hp, ya.reshape(bsz * seq, D_A), yb.reshape(bsz * seq, D_B), woa, wob, g_ffn,
                          wg, wu, wd)
            conv_p.append(cb_new)
            re_p.append(hr.reshape(bsz, S5_G, S5_P))
            im_p.append(hi.reshape(bsz, S5_G, S5_P))

            proj = _norm_matmul(hs, g_mix, ab_w_in[j].astype(F32), tn=w_in.shape[1])
            ya, nb_t = _conv_sample(proj, jnp.transpose(state_conv[j].astype(F32), (1, 0, 2)), cw, cb, clg, clb)
            yb_t, hr, hi = _s5(proj[:, 2 * D_A:].reshape(1, n_s, D_B),
                               state_ssm_re[j].astype(F32).reshape(n_s, S5_STATE),
                               state_ssm_im[j].astype(F32).reshape(n_s, S5_STATE),
                               abar_re, abar_im, bcat, ccat, sd, sgw, sgb, tt=1, seq_major=False)
            hs = _mix_ffn(hs, ya, yb_t.reshape(n_s, D_B), ab_w_out[j, :D_A].astype(F32),
                          ab_w_out[j, D_A:].astype(F32), g_ffn, jnp.pad(ffn_w_gate[j].astype(F32), pad),
                          jnp.pad(ffn_w_up[j].astype(F32), pad),
                          jnp.pad(ffn_w_down[j].astype(F32), (pad[1], pad[0])))
            conv_s.append(jnp.transpose(nb_t, (1, 0, 2)))
            re_s.append(hr.reshape(n_s, S5_G, S5_P))
            im_s.append(hi.reshape(n_s, S5_G, S5_P))
        else:
            w_in = hg_w_in[j].astype(BF16)
            wo = hg_w_out[j].astype(BF16)
            lb, gn = row(lbs[j]), row(hg_gnorm[j])
            router_pad = jnp.pad(moe_router[j].astype(F32), ((0, 0), (0, LANES - N_EXP)))
            r_hi = router_pad.astype(BF16)
            r_lo = (router_pad - r_hi.astype(F32)).astype(BF16)

            o, s_new = _hgrn_prompt(hp.reshape(bsz, seq, d), g_mix, w_in, lb, gn)
            h1_p, x_all, ids_p, gts_p = _mix_router(hp, o.reshape(bsz * seq, d), wo, g_ffn, r_hi, r_lo,
                                                    x_all, 0)
            hg_p.append(s_new)

            proj = _norm_matmul(hs, g_mix, hg_w_in[j].astype(F32), tn=1024)
            o, s_new = _hgrn_sample(proj, state_hgrn.astype(F32), j, lb, gn)
            h1_s, x_all, ids_s, gts_s = _mix_router(hs, o, hg_w_out[j].astype(F32), g_ffn, r_hi, r_lo, x_all,
                                                    bsz * seq)
            hg_s.append(s_new)

            n_rows = (2 * x_all.shape[0] + N_EXP * (MOE_TM - 1)) // MOE_TM * MOE_TM
            hp, hs = _moe(h1_p, h1_s, x_all, ids_p, ids_s, gts_p, gts_s, jnp.zeros((n_rows, d), F32),
                          moe_wg, moe_wu, moe_wd, j, g_fin, l == depth - 1)

    if depth % 2 == 1:
        hp, hs = _final_norm(hp, g_fin), _final_norm(hs, g_fin)
    y_prompt = hp.reshape(bsz, seq, d)
    y_sample = hs.reshape(n_s, 1, d)
    return (y_prompt, y_sample, jnp.stack(conv_p), jnp.stack(re_p), jnp.stack(im_p), jnp.stack(hg_p),
            jnp.stack(conv_s), jnp.stack(re_s), jnp.stack(im_s), jnp.stack(hg_s))
```

```python
import functools
import math

import jax
import jax.numpy as jnp
from jax import lax
from jax.experimental import pallas as pl
from jax.experimental.pallas import tpu as pltpu

F32 = jnp.float32
BF16 = jnp.bfloat16
I32 = jnp.int32

EPS = 1e-6
LN_EPS = 1e-5
FORGET_FLOOR = 1e-30

D_MODEL = 1024
D_A = 512
D_B = 512
CONV_W = 31
CONV_HALO = 32
S5_G = 32
S5_P = 64
S5_GH = 16
S5_STATE = S5_G * S5_P
S5_CHUNKS = 4
HG_HEADS = 8
HG_K = 128
HG_V = 128
HG_CHUNK = 64
D_FF = 2752
D_FF_PAD = 2816
N_EXP = 8
D_EXP = 3584
LANES = 128
VMEM_LIMIT_BYTES = 56 * 1024 * 1024
EXP_RANGE = 80.0

MOE_TM = 1024


def _cparams(*sem):
    return pltpu.CompilerParams(dimension_semantics=sem, vmem_limit_bytes=VMEM_LIMIT_BYTES)


def _sigmoid(x):
    return 1.0 / (1.0 + jnp.exp(-x))


def _silu(x):
    return x * _sigmoid(x)


def _rms(x, g):
    return x * lax.rsqrt(jnp.mean(x * x, axis=-1, keepdims=True) + EPS) * g


def _pdot(a, b, precise):
    if precise:
        return jnp.dot(a.astype(F32), b.astype(F32), preferred_element_type=F32, precision=lax.Precision.HIGHEST)
    return jnp.dot(a.astype(BF16), b.astype(BF16), preferred_element_type=F32)


def _norm_matmul_kernel(x_ref, g_ref, w_ref, o_ref, xn_ref, *, precise):
    @pl.when(pl.program_id(1) == 0)
    def _():
        xn_ref[...] = _rms(x_ref[...], g_ref[...]).astype(xn_ref.dtype)

    o_ref[...] = _pdot(xn_ref[...], w_ref[...], precise).astype(o_ref.dtype)


def _norm_matmul(x, g, w, tn, out_dtype=F32):
    precise = w.dtype == F32
    n, d = x.shape
    nout = w.shape[1]
    tm = min(1024, n)
    return pl.pallas_call(
        functools.partial(_norm_matmul_kernel, precise=precise),
        grid=(n // tm, nout // tn),
        in_specs=[
            pl.BlockSpec((tm, d), lambda i, j: (i, 0)),
            pl.BlockSpec((1, d), lambda i, j: (0, 0)),
            pl.BlockSpec((d, tn), lambda i, j: (0, j)),
        ],
        out_specs=pl.BlockSpec((tm, tn), lambda i, j: (i, j)),
        out_shape=jax.ShapeDtypeStruct((n, nout), out_dtype),
        scratch_shapes=[pltpu.VMEM((tm, d), F32 if precise else BF16)],
        compiler_params=_cparams("parallel", "arbitrary"),
        name="norm_matmul",
    )(x, g, w)


def _conv_post(y, b, lg, lb):
    y = y + b
    mu = jnp.mean(y, axis=-1, keepdims=True)
    yc = y - mu
    var = jnp.mean(yc * yc, axis=-1, keepdims=True)
    return _silu(yc * lax.rsqrt(var + LN_EPS) * lg + lb)


def _conv_prompt_kernel(av_ref, ag_ref, w_ref, b_ref, lg_ref, lb_ref, y_ref, nb_ref, ubuf, *, tt, rb):
    t = pl.program_id(1)

    @pl.when(t == 0)
    def _():
        ubuf[0:CONV_HALO, :] = jnp.zeros((CONV_HALO, D_A), F32)

    @pl.when(t > 0)
    def _():
        ubuf[0:CONV_HALO, :] = ubuf[tt:tt + CONV_HALO, :]

    ubuf[CONV_HALO:CONV_HALO + tt, :] = av_ref[0].astype(F32) * _sigmoid(ag_ref[0].astype(F32))
    off = CONV_HALO - (CONV_W - 1)
    span = rb + CONV_HALO
    for r in range(tt // rb):
        win = ubuf[r * rb:r * rb + span, :]
        acc = jnp.zeros((rb, D_A), F32)
        for p in range(8):
            wp = win if p == 0 else pltpu.roll(win, span - p, 0)
            for q in range(CONV_HALO // 8 + 1):
                k = 8 * q + p - off
                if 0 <= k < CONV_W:
                    acc = acc + w_ref[k:k + 1, :] * wp[8 * q:8 * q + rb, :]
        y_ref[0, r * rb:(r + 1) * rb, :] = _conv_post(acc, b_ref[...], lg_ref[...], lb_ref[...]).astype(y_ref.dtype)

    @pl.when(t == pl.num_programs(1) - 1)
    def _():
        nb_ref[0] = ubuf[tt + off:tt + CONV_HALO, :]


def _conv_prompt(proj3, w, b, lg, lb):
    bsz, t, _ = proj3.shape
    tt = min(256, t)
    kern = functools.partial(_conv_prompt_kernel, tt=tt, rb=32)
    vec = pl.BlockSpec((1, D_A), lambda i, j: (0, 0))
    return pl.pallas_call(
        kern,
        grid=(bsz, t // tt),
        in_specs=[
            pl.BlockSpec((1, tt, D_A), lambda i, j: (i, j, 0)),
            pl.BlockSpec((1, tt, D_A), lambda i, j: (i, j, 1)),
            pl.BlockSpec((CONV_W, D_A), lambda i, j: (0, 0)),
            vec, vec, vec,
        ],
        out_specs=[
            pl.BlockSpec((1, tt, D_A), lambda i, j: (i, j, 0)),
            pl.BlockSpec((1, CONV_W - 1, D_A), lambda i, j: (i, 0, 0)),
        ],
        out_shape=[
            jax.ShapeDtypeStruct((bsz, t, D_A), BF16),
            jax.ShapeDtypeStruct((bsz, CONV_W - 1, D_A), F32),
        ],
        scratch_shapes=[pltpu.VMEM((CONV_HALO + tt, D_A), F32)],
        compiler_params=_cparams("parallel", "arbitrary"),
        name="conv_prompt",
    )(proj3, proj3, w, b, lg, lb)


def _conv_sample_kernel(av_ref, ag_ref, buf_ref, w_ref, b_ref, lg_ref, lb_ref, y_ref, nb_ref):
    u = av_ref[...] * _sigmoid(ag_ref[...])
    acc = w_ref[CONV_W - 1:CONV_W, :] * u
    for k in range(CONV_W - 1):
        acc = acc + w_ref[k:k + 1, :] * buf_ref[k]
    y_ref[...] = _conv_post(acc, b_ref[...], lg_ref[...], lb_ref[...]).astype(y_ref.dtype)
    for k in range(CONV_W - 2):
        nb_ref[k] = buf_ref[k + 1]
    nb_ref[CONV_W - 2] = u


def _conv_sample(proj, buf_t, w, b, lg, lb):
    n = proj.shape[0]
    nbk = min(32, n)
    vec = pl.BlockSpec((1, D_A), lambda i: (0, 0))
    return pl.pallas_call(
        _conv_sample_kernel,
        grid=(n // nbk,),
        in_specs=[
            pl.BlockSpec((nbk, D_A), lambda i: (i, 0)),
            pl.BlockSpec((nbk, D_A), lambda i: (i, 1)),
            pl.BlockSpec((CONV_W - 1, nbk, D_A), lambda i: (0, i, 0)),
            pl.BlockSpec((CONV_W, D_A), lambda i: (0, 0)),
            vec, vec, vec,
        ],
        out_specs=[
            pl.BlockSpec((nbk, D_A), lambda i: (i, 0)),
            pl.BlockSpec((CONV_W - 1, nbk, D_A), lambda i: (0, i, 0)),
        ],
        out_shape=[
            jax.ShapeDtypeStruct((n, D_A), F32),
            jax.ShapeDtypeStruct((CONV_W - 1, n, D_A), F32),
        ],
        compiler_params=_cparams("parallel"),
        name="conv_sample",
    )(proj, proj, buf_t, w, b, lg, lb)


def _s5_kernel(u_ref, h0r_ref, h0i_ref, ar_ref, ai_ref, bcat_ref, ccat_ref, d_ref, gw_ref, gb_ref,
               y_ref, hr_out, hi_out, sre, sim, cre, cim, rt, *, tt, nb, lw, seq_major):
    i = pl.program_id(0)
    m = tt * nb
    cw = S5_STATE // S5_CHUNKS

    @pl.when(i == 0)
    def _():
        cre[...] = h0r_ref[...]
        cim[...] = h0i_ref[...]

    if seq_major:
        for b in range(nb):
            rt[:, b, :] = u_ref[b].astype(F32)
        u = rt[...].reshape(m, D_B)
    else:
        u = u_ref[...].reshape(m, D_B)
    ub = u.astype(BF16)
    for c in range(S5_CHUNKS):
        bu = jnp.dot(ub[:, c * LANES:(c + 1) * LANES], bcat_ref[c], preferred_element_type=F32)
        sre[:, :, c * cw:(c + 1) * cw] = bu[:, :cw].reshape(tt, nb, cw)
        sim[:, :, c * cw:(c + 1) * cw] = bu[:, cw:].reshape(tt, nb, cw)

    for c in range(S5_STATE // lw):
        ls = slice(c * lw, (c + 1) * lw)
        ar = jnp.broadcast_to(ar_ref[:, ls], (nb, lw))
        ai = jnp.broadcast_to(ai_ref[:, ls], (nb, lw))

        def body(t, carry, ls=ls, ar=ar, ai=ai):
            hr, hi = carry
            nr = ar * hr - ai * hi + sre[t, :, ls]
            ni = ar * hi + ai * hr + sim[t, :, ls]
            sre[t, :, ls] = nr
            sim[t, :, ls] = ni
            return nr, ni

        hr, hi = lax.fori_loop(0, tt, body, (cre[:, ls], cim[:, ls]), unroll=min(tt, 8))
        cre[:, ls] = hr
        cim[:, ls] = hi

    hre = sre[...].reshape(m, S5_STATE)
    him = sim[...].reshape(m, S5_STATE)
    ys = []
    for c in range(S5_CHUNKS):
        hcat = jnp.concatenate([hre[:, c * cw:(c + 1) * cw], him[:, c * cw:(c + 1) * cw]], axis=1)
        ys.append(jnp.dot(hcat.astype(BF16), ccat_ref[c], preferred_element_type=F32))
    y = jnp.concatenate(ys, axis=1) + d_ref[...] * u
    y = 0.5 * y * (1.0 + lax.erf(y * (1.0 / math.sqrt(2.0))))
    z = jnp.dot(y.astype(BF16), gw_ref[...], preferred_element_type=F32) + gb_ref[...]
    out = (y * _sigmoid(z)).reshape(tt, nb, D_B)
    if seq_major:
        rt[...] = out
        for b in range(nb):
            y_ref[b] = rt[:, b, :].astype(y_ref.dtype)
    else:
        y_ref[...] = out.astype(y_ref.dtype)

    @pl.when(i == pl.num_programs(0) - 1)
    def _():
        hr_out[...] = cre[...]
        hi_out[...] = cim[...]


def _s5(u, h0r, h0i, abar_re, abar_im, bcat, ccat, d, glu_w, glu_b, tt, seq_major):
    if seq_major:
        nb, t, c = u.shape
        u_spec = pl.BlockSpec((nb, tt, D_B), lambda i: (0, i, c // D_B - 1))
        y_spec = pl.BlockSpec((nb, tt, D_B), lambda i: (0, i, 0))
        y_shape = (nb, t, D_B)
    else:
        t, nb, _ = u.shape
        u_spec = pl.BlockSpec((tt, nb, D_B), lambda i: (i, 0, 0))
        y_spec = u_spec
        y_shape = (t, nb, D_B)
    lw = max(LANES, min(512, 8 * 1024 // nb))
    kern = functools.partial(_s5_kernel, tt=tt, nb=nb, lw=lw, seq_major=seq_major)
    full = lambda shape: pl.BlockSpec(shape, lambda i: (0,) * len(shape))
    return pl.pallas_call(
        kern,
        grid=(t // tt,),
        in_specs=[
            u_spec,
            full((nb, S5_STATE)), full((nb, S5_STATE)),
            full((1, S5_STATE)), full((1, S5_STATE)),
            full(bcat.shape), full(ccat.shape),
            full((1, D_B)), full((D_B, D_B)), full((1, D_B)),
        ],
        out_specs=[
            y_spec,
            full((nb, S5_STATE)), full((nb, S5_STATE)),
        ],
        out_shape=[
            jax.ShapeDtypeStruct(y_shape, BF16 if seq_major else F32),
            jax.ShapeDtypeStruct((nb, S5_STATE), F32),
            jax.ShapeDtypeStruct((nb, S5_STATE), F32),
        ],
        scratch_shapes=[
            pltpu.VMEM((tt, nb, S5_STATE), F32), pltpu.VMEM((tt, nb, S5_STATE), F32),
            pltpu.VMEM((nb, S5_STATE), F32), pltpu.VMEM((nb, S5_STATE), F32),
            pltpu.VMEM((tt, nb, D_B), F32),
        ],
        compiler_params=_cparams("arbitrary"),
        name="s5",
    )(u, h0r, h0i, abar_re, abar_im, bcat, ccat, d, glu_w, glu_b)


def _s5_params(a_re, a_im, log_dt, b_re, b_im, c_re, c_im):
    dt = jnp.exp(log_dt)[:, None]
    mag = jnp.exp(dt * a_re)
    ang = dt * a_im
    abar_re, abar_im = mag * jnp.cos(ang), mag * jnp.sin(ang)
    den = a_re * a_re + a_im * a_im
    nr, ni = abar_re - 1.0, abar_im
    coef_re = (nr * a_re + ni * a_im) / den
    coef_im = (ni * a_re - nr * a_im) / den
    bbar_re = coef_re[..., None] * b_re - coef_im[..., None] * b_im
    bbar_im = coef_re[..., None] * b_im + coef_im[..., None] * b_re
    gpc = S5_G // S5_CHUNKS
    eye = jnp.eye(gpc, dtype=F32)

    def bblk(x):
        x = x.reshape(S5_CHUNKS, gpc, S5_P, S5_GH)
        return jnp.einsum("cgph,gk->cghkp", x, eye).reshape(S5_CHUNKS, gpc * S5_GH, gpc * S5_P)

    def cblk(x):
        x = x.reshape(S5_CHUNKS, gpc, S5_GH, S5_P)
        return jnp.einsum("cghp,gk->cgpkh", x, eye).reshape(S5_CHUNKS, gpc * S5_P, gpc * S5_GH)

    bcat = jnp.concatenate([bblk(bbar_re), bblk(bbar_im)], axis=2).astype(BF16)
    ccat = jnp.concatenate([cblk(c_re), -cblk(c_im)], axis=1).astype(BF16)
    return abar_re.reshape(1, S5_STATE), abar_im.reshape(1, S5_STATE), bcat, ccat


def _hg_gates(q, f, lb):
    qf = _silu(q)
    sig = _sigmoid(f)
    forget = lb + (1.0 - lb) * sig
    logg = jnp.log(jnp.maximum(forget, FORGET_FLOOR))
    kf = (1.0 - lb) * (1.0 - sig)
    return qf, logg, kf


def _hg_out(o, g, gn):
    return o * lax.rsqrt(jnp.mean(o * o, axis=-1, keepdims=True) + EPS) * gn * _silu(g)


def _cumsum_rows(x, tri):
    hi = x.astype(BF16)
    r1 = x - hi.astype(F32)
    mid = r1.astype(BF16)
    lo = (r1 - mid.astype(F32)).astype(BF16)
    dot = lambda p: jnp.dot(tri, p, preferred_element_type=F32)
    return dot(hi) + dot(mid) + dot(lo)


def _hgrn_prompt_kernel(xc_ref, xn_ref, nm_ref, w_ref, lb_ref, gn_ref, o_ref, s_out, st, obuf, pbuf, *, tt):
    t = pl.program_id(1)
    L = HG_CHUNK
    dc = HG_HEADS * HG_K
    step = pl.program_id(0) * pl.num_programs(1) + t
    slot = step % 2

    def project(x_ref):
        xn = _rms(x_ref[0], nm_ref[...]).astype(BF16)
        return jnp.dot(xn, w_ref[...], preferred_element_type=F32).astype(BF16)

    def project_next():
        pbuf[1 - slot] = project(xn_ref)

    @pl.when(step == 0)
    def _():
        pbuf[0] = project(xc_ref)

    @pl.when(t == 0)
    def _():
        st[...] = jnp.zeros(st.shape, F32)

    q_of = lambda rs: pbuf[slot, rs, 0:dc]
    f_of = lambda rs: pbuf[slot, rs, dc:2 * dc]
    v_of = lambda rs: pbuf[slot, rs, 2 * dc:3 * dc]
    g_of = lambda rs, hs: pbuf[slot, rs, 3 * dc + hs.start:3 * dc + hs.stop]

    row = lax.broadcasted_iota(I32, (L, L), 0)
    col = lax.broadcasted_iota(I32, (L, L), 1)
    causal = row >= col
    tri = causal.astype(BF16)
    nt = (((1,), (1,)), ((), ()))
    tn = (((0,), (0,)), ((), ()))
    chunks = []
    spread = None
    for c in range(tt // L):
        rs = slice(c * L, (c + 1) * L)
        q, logg, k = _hg_gates(q_of(rs).astype(F32), f_of(rs).astype(F32), lb_ref[...])
        vb = v_of(rs)
        gc = _cumsum_rows(logg, tri)
        gmid = gc[L // 2 - 1:L // 2, :]
        glast = gc[L - 1:L, :]
        sp = jnp.maximum(gc[0:1, :] - gmid, gmid - glast)
        spread = sp if spread is None else jnp.maximum(spread, sp)
        chunks.append((rs, q, logg, k, vb, gc, gmid, glast))
    fast = jnp.max(spread) <= EXP_RANGE

    @pl.when(fast)
    def _():
        for rs, q, logg, k, vb, gc, gmid, glast in chunks:
            qe = (q * jnp.exp(gc - gmid)).astype(BF16)
            ke = (k * jnp.exp(gmid - gc)).astype(BF16)
            qg = (q * jnp.exp(gc)).astype(BF16)
            kd = (k * jnp.exp(glast - gc)).astype(BF16)
            dlast = jnp.exp(glast)
            for h in range(HG_HEADS):
                hs = slice(h * HG_K, (h + 1) * HG_K)
                sc = lax.dot_general(qe[:, hs], ke[:, hs], nt, preferred_element_type=F32)
                sc = jnp.where(causal, sc, 0.0).astype(BF16)
                s_t = st[h]
                o = (jnp.dot(sc, vb[:, hs], preferred_element_type=F32)
                     + lax.dot_general(qg[:, hs], s_t.astype(BF16), nt, preferred_element_type=F32))
                st[h] = dlast[:, hs] * s_t + lax.dot_general(vb[:, hs], kd[:, hs], tn,
                                                             preferred_element_type=F32)
                o_ref[0, rs, hs] = _hg_out(o, g_of(rs, hs).astype(F32), gn_ref[...]).astype(o_ref.dtype)
        project_next()

    @pl.when(jnp.logical_not(fast))
    def _():
        for rs, q, logg, k, vb, gc, gmid, glast in chunks:
            dec = jnp.exp(logg)
            kb = k.astype(BF16)
            rsel = lax.broadcasted_iota(I32, (L, 1), 0)
            obuf[...] = jnp.zeros(obuf.shape, F32)

            def step(t, carry):
                sel = rsel == t
                g_t = jnp.sum(jnp.where(sel, dec, 0.0), axis=0, keepdims=True)
                v_t = jnp.where(sel, vb, jnp.zeros_like(vb))
                q_t = jnp.where(sel, q, 0.0).astype(BF16)
                for h in range(HG_HEADS):
                    hs = slice(h * HG_K, (h + 1) * HG_K)
                    s_new = g_t[:, hs] * st[h] + lax.dot_general(v_t[:, hs], kb[:, hs], tn,
                                                                 preferred_element_type=F32)
                    st[h] = s_new
                    obuf[:, hs] += lax.dot_general(q_t[:, hs], s_new.astype(BF16), nt,
                                                   preferred_element_type=F32)
                return carry

            lax.fori_loop(0, L, step, 0)
            for h in range(HG_HEADS):
                hs = slice(h * HG_K, (h + 1) * HG_K)
                o_ref[0, rs, hs] = _hg_out(obuf[:, hs], g_of(rs, hs).astype(F32),
                                           gn_ref[...]).astype(o_ref.dtype)
        project_next()

    @pl.when(t == pl.num_programs(1) - 1)
    def _():
        for h in range(HG_HEADS):
            s_out[0, h] = st[h].T


def _hgrn_prompt(h3, g_mix, w_in, lb, gn):
    bsz, t, d = h3.shape
    tt = min(256, t)
    nt = t // tt
    dc = HG_HEADS * HG_K
    kern = functools.partial(_hgrn_prompt_kernel, tt=tt)

    def nxt(i, j):
        f = jnp.minimum(i * nt + j + 1, bsz * nt - 1)
        return (f // nt, f % nt, 0)

    return pl.pallas_call(
        kern,
        grid=(bsz, nt),
        in_specs=[pl.BlockSpec((1, tt, d), lambda i, j: (i, j, 0)),
                  pl.BlockSpec((1, tt, d), nxt),
                  pl.BlockSpec((1, d), lambda i, j: (0, 0)),
                  pl.BlockSpec((d, 4 * dc), lambda i, j: (0, 0)),
                  pl.BlockSpec((1, dc), lambda i, j: (0, 0)),
                  pl.BlockSpec((1, HG_V), lambda i, j: (0, 0))],
        out_specs=[
            pl.BlockSpec((1, tt, dc), lambda i, j: (i, j, 0)),
            pl.BlockSpec((1, HG_HEADS, HG_K, HG_V), lambda i, j: (i, 0, 0, 0)),
        ],
        out_shape=[
            jax.ShapeDtypeStruct((bsz, t, dc), BF16),
            jax.ShapeDtypeStruct((bsz, HG_HEADS, HG_K, HG_V), F32),
        ],
        scratch_shapes=[pltpu.VMEM((HG_HEADS, HG_V, HG_K), F32), pltpu.VMEM((HG_CHUNK, dc), F32),
                        pltpu.VMEM((2, tt, 4 * dc), BF16)],
        compiler_params=_cparams("arbitrary", "arbitrary"),
        name="hgrn_prompt",
    )(h3, h3, g_mix, w_in, lb, gn)


def _hgrn_sample_kernel(q_ref, f_ref, v_ref, g_ref, s_ref, lb_ref, gn_ref, *rest, nbk, layer):
    o_ref, s_out = rest[-2:]
    slot = 0
    if s_out.shape[0] > 1:
        slot = layer
        for l in range(s_out.shape[0]):
            if l != layer:
                s_out[l] = jnp.zeros(s_out.shape[1:], F32)
    for h in range(HG_HEADS):
        hs = slice(h * HG_K, (h + 1) * HG_K)
        q, logg, k = _hg_gates(q_ref[:, hs], f_ref[:, hs], lb_ref[:, hs])
        dec = jnp.exp(logg)
        v = v_ref[:, hs]
        rows = []
        for n in range(nbk):
            col = lambda x: jnp.broadcast_to(x[n:n + 1, :], (HG_K, HG_K)).T
            s_new = col(dec) * s_ref[0, n, h] + col(k) * v[n:n + 1, :]
            s_out[slot, n, h] = s_new
            rows.append(jnp.sum(col(q) * s_new, axis=0, keepdims=True))
        o = jnp.concatenate(rows, axis=0)
        o_ref[:, hs] = _hg_out(o, g_ref[:, hs], gn_ref[...]).astype(o_ref.dtype)


def _hgrn_sample(proj, s_all, layer, lb, gn, s_new_all=None):
    n = proj.shape[0]
    nbk = 8
    dc = HG_HEADS * HG_K
    n_layers = s_all.shape[0]
    kern = functools.partial(_hgrn_sample_kernel, nbk=nbk, layer=layer)
    blk = lambda c: pl.BlockSpec((nbk, dc), lambda i, c=c: (i, c))
    state_blk = pl.BlockSpec((1, nbk, HG_HEADS, HG_K, HG_V), lambda i: (layer, i, 0, 0, 0))
    in_specs = [blk(0), blk(1), blk(2), blk(3), state_blk,
                pl.BlockSpec((1, dc), lambda i: (0, 0)),
                pl.BlockSpec((1, HG_V), lambda i: (0, 0))]
    args = [proj, proj, proj, proj, s_all, lb, gn]
    if s_new_all is None:
        out_state, aliases = pl.BlockSpec((n_layers, nbk, HG_HEADS, HG_K, HG_V), lambda i: (0, i, 0, 0, 0)), {}
    else:
        in_specs.append(pl.BlockSpec(memory_space=pl.ANY))
        args.append(s_new_all)
        out_state, aliases = state_blk, {len(args) - 1: 1}
    return pl.pallas_call(
        kern,
        grid=(n // nbk,),
        in_specs=in_specs,
        out_specs=[pl.BlockSpec((nbk, dc), lambda i: (i, 0)), out_state],
        out_shape=[
            jax.ShapeDtypeStruct((n, dc), F32),
            jax.ShapeDtypeStruct(s_all.shape, F32),
        ],
        input_output_aliases=aliases,
        compiler_params=_cparams("parallel"),
        name="hgrn_sample",
    )(*args)


def _mix_ffn_kernel(h_ref, ya_ref, yb_ref, woa_ref, wob_ref, g_ref, wg_ref, wu_ref, wd_ref, o_ref,
                    h1_ref, xn_ref, acc_ref, *, precise):
    k = pl.program_id(1)

    @pl.when(k == 0)
    def _():
        h1 = (h_ref[...] + _pdot(ya_ref[...], woa_ref[...], precise)
              + _pdot(yb_ref[...], wob_ref[...], precise))
        h1_ref[...] = h1
        xn_ref[...] = _rms(h1, g_ref[...]).astype(xn_ref.dtype)
        acc_ref[...] = jnp.zeros(acc_ref.shape, F32)

    xn = xn_ref[...]
    a = _pdot(xn, wg_ref[...], precise)
    b = _pdot(xn, wu_ref[...], precise)
    acc_ref[...] += _pdot(_silu(a) * b, wd_ref[...], precise)

    @pl.when(k == pl.num_programs(1) - 1)
    def _():
        o_ref[...] = h1_ref[...] + acc_ref[...]


def _mix_ffn(h, ya, yb, woa, wob, g, wg, wu, wd):
    precise = wg.dtype == F32
    n, d = h.shape
    tm = min(512, n)
    tf = D_FF_PAD // 2
    row = lambda w: pl.BlockSpec((tm, w), lambda i, k: (i, 0))
    return pl.pallas_call(
        functools.partial(_mix_ffn_kernel, precise=precise),
        grid=(n // tm, D_FF_PAD // tf),
        in_specs=[
            row(d), row(D_A), row(D_B),
            pl.BlockSpec((D_A, d), lambda i, k: (0, 0)),
            pl.BlockSpec((D_B, d), lambda i, k: (0, 0)),
            pl.BlockSpec((1, d), lambda i, k: (0, 0)),
            pl.BlockSpec((d, tf), lambda i, k: (0, k)),
            pl.BlockSpec((d, tf), lambda i, k: (0, k)),
            pl.BlockSpec((tf, d), lambda i, k: (k, 0)),
        ],
        out_specs=row(d),
        out_shape=jax.ShapeDtypeStruct((n, d), F32),
        scratch_shapes=[pltpu.VMEM((tm, d), F32), pltpu.VMEM((tm, d), F32 if precise else BF16),
                        pltpu.VMEM((tm, d), F32)],
        compiler_params=_cparams("parallel", "arbitrary"),
        name="mix_ffn",
    )(h, ya, yb, woa, wob, g, wg, wu, wd)


def _mix_router_kernel(h_ref, y_ref, wo_ref, g_ref, rh_ref, rl_ref, *rest, precise):
    h1_ref, xn_ref, ids_ref, gts_ref = rest[-4:]
    h1 = h_ref[...] + _pdot(y_ref[...], wo_ref[...], precise)
    h1_ref[...] = h1
    xn = _rms(h1, g_ref[...])
    xn_ref[...] = xn
    xh = xn.astype(BF16)
    xl = (xn - xh.astype(F32)).astype(BF16)
    dot = lambda a, b: jnp.dot(a, b, preferred_element_type=F32)
    logits = dot(xh, rh_ref[...]) + (dot(xh, rl_ref[...]) + dot(xl, rh_ref[...]))
    lane = lax.broadcasted_iota(I32, logits.shape, 1)
    neg = jnp.float32(-jnp.inf)
    logits = jnp.where(lane < N_EXP, logits, neg)
    m1 = jnp.max(logits, axis=-1, keepdims=True)
    i1 = jnp.min(jnp.where(logits == m1, lane, LANES), axis=-1, keepdims=True)
    rest = jnp.where(lane == i1, neg, logits)
    m2 = jnp.max(rest, axis=-1, keepdims=True)
    i2 = jnp.min(jnp.where(rest == m2, lane, LANES), axis=-1, keepdims=True)
    e2 = jnp.exp(m2 - m1)
    g1 = 1.0 / (1.0 + e2)
    g2 = e2 / (1.0 + e2)
    ids = jnp.where(lane == 0, i1, jnp.where(lane == 1, i2, 0))
    ids_ref[...] = ids.T[0:8, :]
    gts_ref[...] = jnp.where(lane == 0, g1, jnp.where(lane == 1, g2, 0.0))


def _mix_router(h, y, wo, g, r_hi, r_lo, xn_all, row0):
    n, d = h.shape
    tm = min(1024, n)
    row = lambda w: pl.BlockSpec((tm, w), lambda i: (i, 0))
    return pl.pallas_call(
        functools.partial(_mix_router_kernel, precise=wo.dtype == F32),
        grid=(n // tm,),
        in_specs=[row(d), row(d),
                  pl.BlockSpec((d, d), lambda i: (0, 0)),
                  pl.BlockSpec((1, d), lambda i: (0, 0)),
                  pl.BlockSpec((d, LANES), lambda i: (0, 0)),
                  pl.BlockSpec((d, LANES), lambda i: (0, 0)),
                  pl.BlockSpec(memory_space=pl.ANY)],
        out_specs=[row(d), pl.BlockSpec((tm, d), lambda i: (i + row0 // tm, 0)),
                   pl.BlockSpec((8, tm), lambda i: (0, i)), row(LANES)],
        out_shape=[
            jax.ShapeDtypeStruct((n, d), F32),
            jax.ShapeDtypeStruct(xn_all.shape, F32),
            jax.ShapeDtypeStruct((8, n), I32),
            jax.ShapeDtypeStruct((n, LANES), F32),
        ],
        input_output_aliases={6: 1},
        compiler_params=_cparams("parallel"),
        name="mix_router",
    )(h, y, wo, g, r_hi, r_lo, xn_all)


def _wait_rows(src, dst, sem, copies):
    for _ in range(copies):
        pltpu.make_async_copy(src, dst, sem).wait()


def _dispatch_kernel(pos_ref, x_ref, xs_in, xs_hbm, sem, *, tt, n_tok):
    del xs_in
    i = pl.program_id(0)

    def body(r, c):
        t = i * tt + r
        for j in range(2):
            pltpu.make_async_copy(x_ref.at[pl.ds(r, 1), :], xs_hbm.at[pl.ds(pos_ref[j * n_tok + t], 1), :],
                                  sem.at[0]).start()
        return c

    lax.fori_loop(0, tt, body, 0, unroll=8)
    _wait_rows(x_ref, xs_hbm.at[pl.ds(0, tt), :], sem.at[0], 2)


def _dispatch(x_all, pos, xs_zero):
    n, d = x_all.shape
    tt = 384 if n % 384 == 0 else LANES
    kern = functools.partial(_dispatch_kernel, tt=tt, n_tok=n)
    grid_spec = pltpu.PrefetchScalarGridSpec(
        num_scalar_prefetch=1,
        grid=(n // tt,),
        in_specs=[pl.BlockSpec((tt, d), lambda i, p: (i, 0)), pl.BlockSpec(memory_space=pl.ANY)],
        out_specs=pl.BlockSpec(memory_space=pl.ANY),
        scratch_shapes=[pltpu.SemaphoreType.DMA((1,))],
    )
    return pl.pallas_call(
        kern,
        grid_spec=grid_spec,
        out_shape=jax.ShapeDtypeStruct(xs_zero.shape, xs_zero.dtype),
        input_output_aliases={2: 0},
        compiler_params=_cparams("arbitrary"),
        name="moe_dispatch",
    )(pos, x_all, xs_zero)


def _expert_kernel(te_ref, nu_ref, xs_ref, wg_ref, wu_ref, wd_ref, ys_ref, xb, acc_ref):
    i = pl.program_id(0)
    k = pl.program_id(1)
    used = i < nu_ref[0]

    @pl.when(k == 0)
    def _():
        acc_ref[...] = jnp.zeros(acc_ref.shape, F32)
        xb[...] = xs_ref[...].astype(BF16)

    @pl.when(used)
    def _():
        x = xb[...]
        a = jnp.dot(x, wg_ref[0, 0].astype(BF16), preferred_element_type=F32)
        b = jnp.dot(x, wu_ref[0, 0].astype(BF16), preferred_element_type=F32)
        acc_ref[...] += jnp.dot((_silu(a) * b).astype(BF16), wd_ref[0, 0].astype(BF16),
                                preferred_element_type=F32)

    @pl.when(k == pl.num_programs(1) - 1)
    def _():
        ys_ref[...] = acc_ref[...]


def _experts(xs, tile_expert, n_used, wg, wu, wd, layer):
    n_rows = xs.shape[0]
    tm = MOE_TM
    nk = 7
    tk = D_EXP // nk

    def kk(i, k, nu):
        return jnp.where(i < nu[0], k, nk - 1)

    grid_spec = pltpu.PrefetchScalarGridSpec(
        num_scalar_prefetch=2,
        grid=(n_rows // tm, nk),
        in_specs=[
            pl.BlockSpec((tm, D_MODEL), lambda i, k, te, nu: (i, 0)),
            pl.BlockSpec((1, 1, D_MODEL, tk), lambda i, k, te, nu: (layer, te[i], 0, kk(i, k, nu))),
            pl.BlockSpec((1, 1, D_MODEL, tk), lambda i, k, te, nu: (layer, te[i], 0, kk(i, k, nu))),
            pl.BlockSpec((1, 1, tk, D_MODEL), lambda i, k, te, nu: (layer, te[i], kk(i, k, nu), 0)),
        ],
        out_specs=pl.BlockSpec((tm, D_MODEL), lambda i, k, te, nu: (i, 0)),
        scratch_shapes=[pltpu.VMEM((tm, D_MODEL), BF16), pltpu.VMEM((tm, D_MODEL), F32)],
    )
    return pl.pallas_call(
        _expert_kernel,
        grid_spec=grid_spec,
        out_shape=jax.ShapeDtypeStruct((n_rows, D_MODEL), F32),
        compiler_params=_cparams("arbitrary", "arbitrary"),
        name="moe_experts",
    )(tile_expert, n_used, xs, wg, wu, wd)


def _combine_kernel(pos_ref, h_ref, g_ref, ys_hbm, fin_ref, o_ref, buf, sem, *, tt, tok0, n_tok, final):
    i = pl.program_id(0)

    def issue(tile, slot):
        def body(r, c):
            t = tok0 + tile * tt + r
            for j in range(2):
                pltpu.make_async_copy(ys_hbm.at[pl.ds(pos_ref[j * n_tok + t], 1), :],
                                      buf.at[slot, j, pl.ds(r, 1), :], sem.at[slot]).start()
            return c

        lax.fori_loop(0, tt, body, 0, unroll=8)

    @pl.when(i == 0)
    def _():
        issue(0, 0)

    @pl.when(i + 1 < pl.num_programs(0))
    def _():
        issue(i + 1, (i + 1) % 2)

    slot = i % 2
    _wait_rows(ys_hbm.at[pl.ds(0, tt), :], buf.at[slot, 0], sem.at[slot], 2)
    g = g_ref[...]
    out = h_ref[...] + (g[:, 0:1] * buf[slot, 0] + g[:, 1:2] * buf[slot, 1])
    o_ref[...] = _rms(out, fin_ref[...]) if final else out


def _combine(h1, gts, pos, ys, tok0, fin_g, final):
    n, d = h1.shape
    tt = min(512, n)
    kern = functools.partial(_combine_kernel, tt=tt, tok0=tok0, n_tok=pos.shape[0] // 2, final=final)
    grid_spec = pltpu.PrefetchScalarGridSpec(
        num_scalar_prefetch=1,
        grid=(n // tt,),
        in_specs=[pl.BlockSpec((tt, d), lambda i, p: (i, 0)),
                  pl.BlockSpec((tt, LANES), lambda i, p: (i, 0)),
                  pl.BlockSpec(memory_space=pl.ANY),
                  pl.BlockSpec((1, d), lambda i, p: (0, 0))],
        out_specs=pl.BlockSpec((tt, d), lambda i, p: (i, 0)),
        scratch_shapes=[pltpu.VMEM((2, 2, tt, d), F32), pltpu.SemaphoreType.DMA((2,))],
    )
    return pl.pallas_call(
        kern,
        grid_spec=grid_spec,
        out_shape=jax.ShapeDtypeStruct((n, d), F32),
        compiler_params=_cparams("arbitrary"),
        name="moe_combine",
    )(pos, h1, gts, ys, fin_g)


def _moe(h1_p, h1_s, x_all, ids_p, ids_s, gts_p, gts_s, xs_zero, wg, wu, wd, layer, fin_g, final):
    n_p = h1_p.shape[0]
    tm = MOE_TM
    n_tiles = xs_zero.shape[0] // tm
    e = jnp.concatenate([ids_p[:2], ids_s[:2]], axis=1)
    experts = jnp.arange(N_EXP, dtype=I32)[:, None]
    is_e = [e[j][None, :] == experts for j in range(2)]
    hit = is_e[0].astype(I32) + is_e[1].astype(I32)
    cum = jnp.cumsum(hit, axis=1)
    rank = cum - hit
    counts = cum[:, -1]
    padded = ((counts + tm - 1) // tm) * tm
    ends = jnp.cumsum(padded)
    offs = ends - padded
    row_of = offs[:, None] + rank
    pos = jnp.stack([jnp.sum(jnp.where(m, row_of, 0), axis=0) for m in is_e]).reshape(-1).astype(I32)
    n_used = (ends[-1] // tm).astype(I32).reshape(1)
    tile_start = jnp.arange(n_tiles, dtype=I32) * tm
    tile_expert = jnp.sum((ends[None, :] <= tile_start[:, None]).astype(I32), axis=1)
    tile_expert = jnp.minimum(tile_expert, N_EXP - 1).astype(I32)

    xs = _dispatch(x_all, pos, xs_zero)
    ys = _experts(xs, tile_expert, n_used, wg, wu, wd, layer)
    out_p = _combine(h1_p, gts_p, pos, ys, 0, fin_g, final)
    out_s = _combine(h1_s, gts_s, pos, ys, n_p, fin_g, final)
    return out_p, out_s


def _final_norm_kernel(x_ref, g_ref, o_ref):
    o_ref[...] = _rms(x_ref[...], g_ref[...])


def _final_norm(x, g):
    n, d = x.shape
    tm = min(1024, n)
    return pl.pallas_call(
        _final_norm_kernel,
        grid=(n // tm,),
        in_specs=[pl.BlockSpec((tm, d), lambda i: (i, 0)), pl.BlockSpec((1, d), lambda i: (0, 0))],
        out_specs=pl.BlockSpec((tm, d), lambda i: (i, 0)),
        out_shape=jax.ShapeDtypeStruct((n, d), F32),
        compiler_params=_cparams("parallel"),
        name="final_norm",
    )(x, g)


def kernel(x_prompt, x_sample, state_conv, state_ssm_re, state_ssm_im, state_hgrn, norm_mix, norm_ffn, norm_final, ab_w_in, ab_w_out, conv_w, conv_b, conv_ln_g, conv_ln_b, ssm_a_re, ssm_a_im, ssm_log_dt, ssm_b_re, ssm_b_im, ssm_c_re, ssm_c_im, ssm_d, ssm_glu_w, ssm_glu_b, hg_w_in, hg_lb_logits, hg_gnorm, hg_w_out, ffn_w_gate, ffn_w_up, ffn_w_down, moe_router, moe_w_gate, moe_w_up, moe_w_down):
    bsz, seq, d = x_prompt.shape
    n_s = x_sample.shape[0]
    depth = norm_mix.shape[0]
    hp = x_prompt.reshape(bsz * seq, d)
    hs = x_sample.reshape(n_s, d)

    lbs = jax.nn.softmax(hg_lb_logits.astype(F32), axis=0)
    lbs = jnp.cumsum(lbs, axis=0) - lbs[0:1]

    conv_p, re_p, im_p, hg_p = [], [], [], []
    conv_s, re_s, im_s, hg_s = [], [], [], None
    zero_state = jnp.zeros((bsz, S5_STATE), F32)
    x_all = jnp.zeros((bsz * seq + n_s, d), F32)
    moe_wg, moe_wu, moe_wd = moe_w_gate.astype(F32), moe_w_up.astype(F32), moe_w_down.astype(F32)
    row = lambda v: v.reshape(1, -1).astype(F32)
    g_fin = row(norm_final)

    for l in range(depth):
        j = l // 2
        g_mix, g_ffn = row(norm_mix[l]), row(norm_ffn[l])
        if l % 2 == 0:
            w_in = ab_w_in[j].astype(BF16)
            woa = ab_w_out[j, :D_A].astype(BF16)
            wob = ab_w_out[j, D_A:].astype(BF16)
            cw, cb, clg, clb = conv_w[j].astype(F32), row(conv_b[j]), row(conv_ln_g[j]), row(conv_ln_b[j])
            abar_re, abar_im, bcat, ccat = _s5_params(
                ssm_a_re[j].astype(F32), ssm_a_im[j].astype(F32), ssm_log_dt[j].astype(F32),
                ssm_b_re[j].astype(F32), ssm_b_im[j].astype(F32), ssm_c_re[j].astype(F32), ssm_c_im[j].astype(F32))
            sd, sgw, sgb = row(ssm_d[j]), ssm_glu_w[j].astype(BF16), row(ssm_glu_b[j])
            pad = ((0, 0), (0, D_FF_PAD - D_FF))
            wg = jnp.pad(ffn_w_gate[j], pad).astype(BF16)
            wu = jnp.pad(ffn_w_up[j], pad).astype(BF16)
            wd = jnp.pad(ffn_w_down[j], (pad[1], pad[0])).astype(BF16)

            proj = _norm_matmul(hp, g_mix, w_in, tn=w_in.shape[1], out_dtype=BF16)
            proj3 = proj.reshape(bsz, seq, -1)
            ya, cb_new = _conv_prompt(proj3, cw, cb, clg, clb)
            yb, hr, hi = _s5(proj3, zero_state, zero_state, abar_re, abar_im, bcat, ccat, sd, sgw, sgb,
                             tt=min(128, seq), seq_major=True)
            hp = _mix_ffn(hp, ya.reshape(bsz * seq, D_A), yb.reshape(bsz * seq, D_B), woa, wob, g_ffn,
                          wg, wu, wd)
            conv_p.append(cb_new)
            re_p.append(hr.reshape(bsz, S5_G, S5_P))
            im_p.append(hi.reshape(bsz, S5_G, S5_P))

            proj = _norm_matmul(hs, g_mix, ab_w_in[j].astype(F32), tn=w_in.shape[1])
            ya, nb_t = _conv_sample(proj, jnp.transpose(state_conv[j].astype(F32), (1, 0, 2)), cw, cb, clg, clb)
            yb_t, hr, hi = _s5(proj[:, 2 * D_A:].reshape(1, n_s, D_B),
                               state_ssm_re[j].astype(F32).reshape(n_s, S5_STATE),
                               state_ssm_im[j].astype(F32).reshape(n_s, S5_STATE),
                               abar_re, abar_im, bcat, ccat, sd, sgw, sgb, tt=1, seq_major=False)
            hs = _mix_ffn(hs, ya, yb_t.reshape(n_s, D_B), ab_w_out[j, :D_A].astype(F32),
                          ab_w_out[j, D_A:].astype(F32), g_ffn, jnp.pad(ffn_w_gate[j].astype(F32), pad),
                          jnp.pad(ffn_w_up[j].astype(F32), pad),
                          jnp.pad(ffn_w_down[j].astype(F32), (pad[1], pad[0])))
            conv_s.append(jnp.transpose(nb_t, (1, 0, 2)))
            re_s.append(hr.reshape(n_s, S5_G, S5_P))
            im_s.append(hi.reshape(n_s, S5_G, S5_P))
        else:
            w_in = hg_w_in[j].astype(BF16)
            wo = hg_w_out[j].astype(BF16)
            lb, gn = row(lbs[j]), row(hg_gnorm[j])
            router_pad = jnp.pad(moe_router[j].astype(F32), ((0, 0), (0, LANES - N_EXP)))
            r_hi = router_pad.astype(BF16)
            r_lo = (router_pad - r_hi.astype(F32)).astype(BF16)

            o, s_new = _hgrn_prompt(hp.reshape(bsz, seq, d), g_mix, w_in, lb, gn)
            h1_p, x_all, ids_p, gts_p = _mix_router(hp, o.reshape(bsz * seq, d), wo, g_ffn, r_hi, r_lo,
                                                    x_all, 0)
            hg_p.append(s_new)

            proj = _norm_matmul(hs, g_mix, hg_w_in[j].astype(F32), tn=1024)
            o, hg_s = _hgrn_sample(proj, state_hgrn.astype(F32), j, lb, gn, hg_s)
            h1_s, x_all, ids_s, gts_s = _mix_router(hs, o, hg_w_out[j].astype(F32), g_ffn, r_hi, r_lo, x_all,
                                                    bsz * seq)

            n_rows = (2 * x_all.shape[0] + N_EXP * (MOE_TM - 1)) // MOE_TM * MOE_TM
            hp, hs = _moe(h1_p, h1_s, x_all, ids_p, ids_s, gts_p, gts_s, jnp.zeros((n_rows, d), F32),
                          moe_wg, moe_wu, moe_wd, j, g_fin, l == depth - 1)

    if depth % 2 == 1:
        hp, hs = _final_norm(hp, g_fin), _final_norm(hs, g_fin)
    y_prompt = hp.reshape(bsz, seq, d)
    y_sample = hs.reshape(n_s, 1, d)
    return (y_prompt, y_sample, jnp.stack(conv_p), jnp.stack(re_p), jnp.stack(im_p), jnp.stack(hg_p),
            jnp.stack(conv_s), jnp.stack(re_s), jnp.stack(im_s), hg_s)
```

```python
import functools
import math

import jax
import jax.numpy as jnp
from jax import lax
from jax.experimental import pallas as pl
from jax.experimental.pallas import tpu as pltpu

F32 = jnp.float32
BF16 = jnp.bfloat16
I32 = jnp.int32

EPS = 1e-6
LN_EPS = 1e-5
FORGET_FLOOR = 1e-30

D_MODEL = 1024
D_A = 512
D_B = 512
CONV_W = 31
CONV_HALO = 32
S5_G = 32
S5_P = 64
S5_GH = 16
S5_STATE = S5_G * S5_P
S5_CHUNKS = 4
HG_HEADS = 8
HG_K = 128
HG_V = 128
HG_CHUNK = 64
D_FF = 2752
D_FF_PAD = 2816
N_EXP = 8
D_EXP = 3584
LANES = 128
VMEM_LIMIT_BYTES = 56 * 1024 * 1024
EXP_RANGE = 80.0

MOE_TM = 1024


def _cparams(*sem):
    return pltpu.CompilerParams(dimension_semantics=sem, vmem_limit_bytes=VMEM_LIMIT_BYTES)


def _sigmoid(x):
    return 1.0 / (1.0 + jnp.exp(-x))


def _silu(x):
    return x * _sigmoid(x)


def _rms(x, g):
    return x * lax.rsqrt(jnp.mean(x * x, axis=-1, keepdims=True) + EPS) * g


def _pdot(a, b, precise):
    if precise:
        return jnp.dot(a.astype(F32), b.astype(F32), preferred_element_type=F32, precision=lax.Precision.HIGHEST)
    return jnp.dot(a.astype(BF16), b.astype(BF16), preferred_element_type=F32)


def _norm_matmul_kernel(x_ref, g_ref, w_ref, o_ref, xn_ref, *, precise):
    @pl.when(pl.program_id(1) == 0)
    def _():
        xn_ref[...] = _rms(x_ref[...], g_ref[...]).astype(xn_ref.dtype)

    o_ref[...] = _pdot(xn_ref[...], w_ref[...], precise).astype(o_ref.dtype)


def _norm_matmul(x, g, w, tn, out_dtype=F32):
    precise = w.dtype == F32
    n, d = x.shape
    nout = w.shape[1]
    tm = min(1024, n)
    return pl.pallas_call(
        functools.partial(_norm_matmul_kernel, precise=precise),
        grid=(n // tm, nout // tn),
        in_specs=[
            pl.BlockSpec((tm, d), lambda i, j: (i, 0)),
            pl.BlockSpec((1, d), lambda i, j: (0, 0)),
            pl.BlockSpec((d, tn), lambda i, j: (0, j)),
        ],
        out_specs=pl.BlockSpec((tm, tn), lambda i, j: (i, j)),
        out_shape=jax.ShapeDtypeStruct((n, nout), out_dtype),
        scratch_shapes=[pltpu.VMEM((tm, d), F32 if precise else BF16)],
        compiler_params=_cparams("parallel", "arbitrary"),
        name="norm_matmul",
    )(x, g, w)


def _conv_post(y, b, lg, lb):
    y = y + b
    mu = jnp.mean(y, axis=-1, keepdims=True)
    yc = y - mu
    var = jnp.mean(yc * yc, axis=-1, keepdims=True)
    return _silu(yc * lax.rsqrt(var + LN_EPS) * lg + lb)


def _conv_prompt_kernel(av_ref, ag_ref, w_ref, b_ref, lg_ref, lb_ref, y_ref, nb_ref, ubuf, *, tt, rb):
    t = pl.program_id(1)

    @pl.when(t == 0)
    def _():
        ubuf[0:CONV_HALO, :] = jnp.zeros((CONV_HALO, D_A), F32)

    @pl.when(t > 0)
    def _():
        ubuf[0:CONV_HALO, :] = ubuf[tt:tt + CONV_HALO, :]

    ubuf[CONV_HALO:CONV_HALO + tt, :] = av_ref[0].astype(F32) * _sigmoid(ag_ref[0].astype(F32))
    off = CONV_HALO - (CONV_W - 1)
    span = rb + CONV_HALO
    for r in range(tt // rb):
        win = ubuf[r * rb:r * rb + span, :]
        acc = jnp.zeros((rb, D_A), F32)
        for p in range(8):
            wp = win if p == 0 else pltpu.roll(win, span - p, 0)
            for q in range(CONV_HALO // 8 + 1):
                k = 8 * q + p - off
                if 0 <= k < CONV_W:
                    acc = acc + w_ref[k:k + 1, :] * wp[8 * q:8 * q + rb, :]
        y_ref[0, r * rb:(r + 1) * rb, :] = _conv_post(acc, b_ref[...], lg_ref[...], lb_ref[...]).astype(y_ref.dtype)

    @pl.when(t == pl.num_programs(1) - 1)
    def _():
        nb_ref[0] = ubuf[tt + off:tt + CONV_HALO, :]


def _conv_prompt(proj3, w, b, lg, lb):
    bsz, t, _ = proj3.shape
    tt = min(256, t)
    kern = functools.partial(_conv_prompt_kernel, tt=tt, rb=32)
    vec = pl.BlockSpec((1, D_A), lambda i, j: (0, 0))
    return pl.pallas_call(
        kern,
        grid=(bsz, t // tt),
        in_specs=[
            pl.BlockSpec((1, tt, D_A), lambda i, j: (i, j, 0)),
            pl.BlockSpec((1, tt, D_A), lambda i, j: (i, j, 1)),
            pl.BlockSpec((CONV_W, D_A), lambda i, j: (0, 0)),
            vec, vec, vec,
        ],
        out_specs=[
            pl.BlockSpec((1, tt, D_A), lambda i, j: (i, j, 0)),
            pl.BlockSpec((1, CONV_W - 1, D_A), lambda i, j: (i, 0, 0)),
        ],
        out_shape=[
            jax.ShapeDtypeStruct((bsz, t, D_A), BF16),
            jax.ShapeDtypeStruct((bsz, CONV_W - 1, D_A), F32),
        ],
        scratch_shapes=[pltpu.VMEM((CONV_HALO + tt, D_A), F32)],
        compiler_params=_cparams("parallel", "arbitrary"),
        name="conv_prompt",
    )(proj3, proj3, w, b, lg, lb)


def _conv_sample_kernel(av_ref, ag_ref, buf_ref, w_ref, b_ref, lg_ref, lb_ref, y_ref, nb_ref):
    u = av_ref[...] * _sigmoid(ag_ref[...])
    acc = w_ref[CONV_W - 1:CONV_W, :] * u
    for k in range(CONV_W - 1):
        acc = acc + w_ref[k:k + 1, :] * buf_ref[k]
    y_ref[...] = _conv_post(acc, b_ref[...], lg_ref[...], lb_ref[...]).astype(y_ref.dtype)
    for k in range(CONV_W - 2):
        nb_ref[k] = buf_ref[k + 1]
    nb_ref[CONV_W - 2] = u


def _conv_sample(proj, buf_t, w, b, lg, lb):
    n = proj.shape[0]
    nbk = min(32, n)
    vec = pl.BlockSpec((1, D_A), lambda i: (0, 0))
    return pl.pallas_call(
        _conv_sample_kernel,
        grid=(n // nbk,),
        in_specs=[
            pl.BlockSpec((nbk, D_A), lambda i: (i, 0)),
            pl.BlockSpec((nbk, D_A), lambda i: (i, 1)),
            pl.BlockSpec((CONV_W - 1, nbk, D_A), lambda i: (0, i, 0)),
            pl.BlockSpec((CONV_W, D_A), lambda i: (0, 0)),
            vec, vec, vec,
        ],
        out_specs=[
            pl.BlockSpec((nbk, D_A), lambda i: (i, 0)),
            pl.BlockSpec((CONV_W - 1, nbk, D_A), lambda i: (0, i, 0)),
        ],
        out_shape=[
            jax.ShapeDtypeStruct((n, D_A), F32),
            jax.ShapeDtypeStruct((CONV_W - 1, n, D_A), F32),
        ],
        compiler_params=_cparams("parallel"),
        name="conv_sample",
    )(proj, proj, buf_t, w, b, lg, lb)


def _s5_kernel(u_ref, h0r_ref, h0i_ref, ar_ref, ai_ref, bcat_ref, ccat_ref, d_ref, gw_ref, gb_ref,
               y_ref, hr_out, hi_out, sre, sim, cre, cim, rt, *, tt, nb, lw, seq_major):
    i = pl.program_id(0)
    m = tt * nb
    cw = S5_STATE // S5_CHUNKS

    @pl.when(i == 0)
    def _():
        cre[...] = h0r_ref[...]
        cim[...] = h0i_ref[...]

    if seq_major:
        for b in range(nb):
            rt[:, b, :] = u_ref[b].astype(F32)
        u = rt[...].reshape(m, D_B)
    else:
        u = u_ref[...].reshape(m, D_B)
    ub = u.astype(BF16)
    for c in range(S5_CHUNKS):
        bu = jnp.dot(ub[:, c * LANES:(c + 1) * LANES], bcat_ref[c], preferred_element_type=F32)
        sre[:, :, c * cw:(c + 1) * cw] = bu[:, :cw].reshape(tt, nb, cw)
        sim[:, :, c * cw:(c + 1) * cw] = bu[:, cw:].reshape(tt, nb, cw)

    for c in range(S5_STATE // lw):
        ls = slice(c * lw, (c + 1) * lw)
        ar = jnp.broadcast_to(ar_ref[:, ls], (nb, lw))
        ai = jnp.broadcast_to(ai_ref[:, ls], (nb, lw))

        def body(t, carry, ls=ls, ar=ar, ai=ai):
            hr, hi = carry
            nr = ar * hr - ai * hi + sre[t, :, ls]
            ni = ar * hi + ai * hr + sim[t, :, ls]
            sre[t, :, ls] = nr
            sim[t, :, ls] = ni
            return nr, ni

        hr, hi = lax.fori_loop(0, tt, body, (cre[:, ls], cim[:, ls]), unroll=min(tt, 8))
        cre[:, ls] = hr
        cim[:, ls] = hi

    hre = sre[...].reshape(m, S5_STATE)
    him = sim[...].reshape(m, S5_STATE)
    ys = []
    for c in range(S5_CHUNKS):
        hcat = jnp.concatenate([hre[:, c * cw:(c + 1) * cw], him[:, c * cw:(c + 1) * cw]], axis=1)
        ys.append(jnp.dot(hcat.astype(BF16), ccat_ref[c], preferred_element_type=F32))
    y = jnp.concatenate(ys, axis=1) + d_ref[...] * u
    y = 0.5 * y * (1.0 + lax.erf(y * (1.0 / math.sqrt(2.0))))
    z = jnp.dot(y.astype(BF16), gw_ref[...], preferred_element_type=F32) + gb_ref[...]
    out = (y * _sigmoid(z)).reshape(tt, nb, D_B)
    if seq_major:
        rt[...] = out
        for b in range(nb):
            y_ref[b] = rt[:, b, :].astype(y_ref.dtype)
    else:
        y_ref[...] = out.astype(y_ref.dtype)

    @pl.when(i == pl.num_programs(0) - 1)
    def _():
        hr_out[...] = cre[...]
        hi_out[...] = cim[...]


def _s5(u, h0r, h0i, abar_re, abar_im, bcat, ccat, d, glu_w, glu_b, tt, seq_major):
    if seq_major:
        nb, t, c = u.shape
        u_spec = pl.BlockSpec((nb, tt, D_B), lambda i: (0, i, c // D_B - 1))
        y_spec = pl.BlockSpec((nb, tt, D_B), lambda i: (0, i, 0))
        y_shape = (nb, t, D_B)
    else:
        t, nb, _ = u.shape
        u_spec = pl.BlockSpec((tt, nb, D_B), lambda i: (i, 0, 0))
        y_spec = u_spec
        y_shape = (t, nb, D_B)
    lw = max(LANES, min(512, 8 * 1024 // nb))
    kern = functools.partial(_s5_kernel, tt=tt, nb=nb, lw=lw, seq_major=seq_major)
    full = lambda shape: pl.BlockSpec(shape, lambda i: (0,) * len(shape))
    return pl.pallas_call(
        kern,
        grid=(t // tt,),
        in_specs=[
            u_spec,
            full((nb, S5_STATE)), full((nb, S5_STATE)),
            full((1, S5_STATE)), full((1, S5_STATE)),
            full(bcat.shape), full(ccat.shape),
            full((1, D_B)), full((D_B, D_B)), full((1, D_B)),
        ],
        out_specs=[
            y_spec,
            full((nb, S5_STATE)), full((nb, S5_STATE)),
        ],
        out_shape=[
            jax.ShapeDtypeStruct(y_shape, BF16 if seq_major else F32),
            jax.ShapeDtypeStruct((nb, S5_STATE), F32),
            jax.ShapeDtypeStruct((nb, S5_STATE), F32),
        ],
        scratch_shapes=[
            pltpu.VMEM((tt, nb, S5_STATE), F32), pltpu.VMEM((tt, nb, S5_STATE), F32),
            pltpu.VMEM((nb, S5_STATE), F32), pltpu.VMEM((nb, S5_STATE), F32),
            pltpu.VMEM((tt, nb, D_B), F32),
        ],
        compiler_params=_cparams("arbitrary"),
        name="s5",
    )(u, h0r, h0i, abar_re, abar_im, bcat, ccat, d, glu_w, glu_b)


def _s5_params(a_re, a_im, log_dt, b_re, b_im, c_re, c_im):
    dt = jnp.exp(log_dt)[:, None]
    mag = jnp.exp(dt * a_re)
    ang = dt * a_im
    abar_re, abar_im = mag * jnp.cos(ang), mag * jnp.sin(ang)
    den = a_re * a_re + a_im * a_im
    nr, ni = abar_re - 1.0, abar_im
    coef_re = (nr * a_re + ni * a_im) / den
    coef_im = (ni * a_re - nr * a_im) / den
    bbar_re = coef_re[..., None] * b_re - coef_im[..., None] * b_im
    bbar_im = coef_re[..., None] * b_im + coef_im[..., None] * b_re
    gpc = S5_G // S5_CHUNKS
    eye = jnp.eye(gpc, dtype=F32)

    def bblk(x):
        x = x.reshape(S5_CHUNKS, gpc, S5_P, S5_GH)
        return jnp.einsum("cgph,gk->cghkp", x, eye).reshape(S5_CHUNKS, gpc * S5_GH, gpc * S5_P)

    def cblk(x):
        x = x.reshape(S5_CHUNKS, gpc, S5_GH, S5_P)
        return jnp.einsum("cghp,gk->cgpkh", x, eye).reshape(S5_CHUNKS, gpc * S5_P, gpc * S5_GH)

    bcat = jnp.concatenate([bblk(bbar_re), bblk(bbar_im)], axis=2).astype(BF16)
    ccat = jnp.concatenate([cblk(c_re), -cblk(c_im)], axis=1).astype(BF16)
    return abar_re.reshape(1, S5_STATE), abar_im.reshape(1, S5_STATE), bcat, ccat


def _hg_gates(q, f, lb):
    qf = _silu(q)
    sig = _sigmoid(f)
    forget = lb + (1.0 - lb) * sig
    logg = jnp.log(jnp.maximum(forget, FORGET_FLOOR))
    kf = (1.0 - lb) * (1.0 - sig)
    return qf, logg, kf


def _hg_out(o, g, gn):
    return o * lax.rsqrt(jnp.mean(o * o, axis=-1, keepdims=True) + EPS) * gn * _silu(g)


def _cumsum_rows(x, tri):
    hi = x.astype(BF16)
    r1 = x - hi.astype(F32)
    mid = r1.astype(BF16)
    lo = (r1 - mid.astype(F32)).astype(BF16)
    dot = lambda p: jnp.dot(tri, p, preferred_element_type=F32)
    return dot(hi) + dot(mid) + dot(lo)


def _hgrn_prompt_kernel(xc_ref, xn_ref, nm_ref, w_ref, lb_ref, gn_ref, o_ref, s_out, st, obuf, pbuf, *, tt):
    t = pl.program_id(1)
    L = HG_CHUNK
    dc = HG_HEADS * HG_K
    step = pl.program_id(0) * pl.num_programs(1) + t
    slot = step % 2

    def project(x_ref):
        xn = _rms(x_ref[0], nm_ref[...]).astype(BF16)
        return jnp.dot(xn, w_ref[...], preferred_element_type=F32).astype(BF16)

    def project_next():
        pbuf[1 - slot] = project(xn_ref)

    @pl.when(step == 0)
    def _():
        pbuf[0] = project(xc_ref)

    @pl.when(t == 0)
    def _():
        st[...] = jnp.zeros(st.shape, F32)

    q_of = lambda rs: pbuf[slot, rs, 0:dc]
    f_of = lambda rs: pbuf[slot, rs, dc:2 * dc]
    v_of = lambda rs: pbuf[slot, rs, 2 * dc:3 * dc]
    g_of = lambda rs, hs: pbuf[slot, rs, 3 * dc + hs.start:3 * dc + hs.stop]

    row = lax.broadcasted_iota(I32, (L, L), 0)
    col = lax.broadcasted_iota(I32, (L, L), 1)
    causal = row >= col
    tri = causal.astype(BF16)
    nt = (((1,), (1,)), ((), ()))
    tn = (((0,), (0,)), ((), ()))
    chunks = []
    spread = None
    for c in range(tt // L):
        rs = slice(c * L, (c + 1) * L)
        q, logg, k = _hg_gates(q_of(rs).astype(F32), f_of(rs).astype(F32), lb_ref[...])
        vb = v_of(rs)
        gc = _cumsum_rows(logg, tri)
        gmid = gc[L // 2 - 1:L // 2, :]
        glast = gc[L - 1:L, :]
        sp = jnp.maximum(gc[0:1, :] - gmid, gmid - glast)
        spread = sp if spread is None else jnp.maximum(spread, sp)
        chunks.append((rs, q, logg, k, vb, gc, gmid, glast))
    fast = jnp.max(spread) <= EXP_RANGE

    @pl.when(fast)
    def _():
        for rs, q, logg, k, vb, gc, gmid, glast in chunks:
            qe = (q * jnp.exp(gc - gmid)).astype(BF16)
            ke = (k * jnp.exp(gmid - gc)).astype(BF16)
            qg = (q * jnp.exp(gc)).astype(BF16)
            kd = (k * jnp.exp(glast - gc)).astype(BF16)
            dlast = jnp.exp(glast)
            for h in range(HG_HEADS):
                hs = slice(h * HG_K, (h + 1) * HG_K)
                sc = lax.dot_general(qe[:, hs], ke[:, hs], nt, preferred_element_type=F32)
                sc = jnp.where(causal, sc, 0.0).astype(BF16)
                s_t = st[h]
                o = (jnp.dot(sc, vb[:, hs], preferred_element_type=F32)
                     + lax.dot_general(qg[:, hs], s_t.astype(BF16), nt, preferred_element_type=F32))
                st[h] = dlast[:, hs] * s_t + lax.dot_general(vb[:, hs], kd[:, hs], tn,
                                                             preferred_element_type=F32)
                o_ref[0, rs, hs] = _hg_out(o, g_of(rs, hs).astype(F32), gn_ref[...]).astype(o_ref.dtype)
        project_next()

    @pl.when(jnp.logical_not(fast))
    def _():
        for rs, q, logg, k, vb, gc, gmid, glast in chunks:
            dec = jnp.exp(logg)
            kb = k.astype(BF16)
            rsel = lax.broadcasted_iota(I32, (L, 1), 0)
            obuf[...] = jnp.zeros(obuf.shape, F32)

            def step(t, carry):
                sel = rsel == t
                g_t = jnp.sum(jnp.where(sel, dec, 0.0), axis=0, keepdims=True)
                v_t = jnp.where(sel, vb, jnp.zeros_like(vb))
                q_t = jnp.where(sel, q, 0.0).astype(BF16)
                for h in range(HG_HEADS):
                    hs = slice(h * HG_K, (h + 1) * HG_K)
                    s_new = g_t[:, hs] * st[h] + lax.dot_general(v_t[:, hs], kb[:, hs], tn,
                                                                 preferred_element_type=F32)
                    st[h] = s_new
                    obuf[:, hs] += lax.dot_general(q_t[:, hs], s_new.astype(BF16), nt,
                                                   preferred_element_type=F32)
                return carry

            lax.fori_loop(0, L, step, 0)
            for h in range(HG_HEADS):
                hs = slice(h * HG_K, (h + 1) * HG_K)
                o_ref[0, rs, hs] = _hg_out(obuf[:, hs], g_of(rs, hs).astype(F32),
                                           gn_ref[...]).astype(o_ref.dtype)
        project_next()

    @pl.when(t == pl.num_programs(1) - 1)
    def _():
        for h in range(HG_HEADS):
            s_out[0, h] = st[h].T


def _hgrn_prompt(h3, g_mix, w_in, lb, gn):
    bsz, t, d = h3.shape
    tt = min(256, t)
    nt = t // tt
    dc = HG_HEADS * HG_K
    kern = functools.partial(_hgrn_prompt_kernel, tt=tt)

    def nxt(i, j):
        f = jnp.minimum(i * nt + j + 1, bsz * nt - 1)
        return (f // nt, f % nt, 0)

    return pl.pallas_call(
        kern,
        grid=(bsz, nt),
        in_specs=[pl.BlockSpec((1, tt, d), lambda i, j: (i, j, 0)),
                  pl.BlockSpec((1, tt, d), nxt),
                  pl.BlockSpec((1, d), lambda i, j: (0, 0)),
                  pl.BlockSpec((d, 4 * dc), lambda i, j: (0, 0)),
                  pl.BlockSpec((1, dc), lambda i, j: (0, 0)),
                  pl.BlockSpec((1, HG_V), lambda i, j: (0, 0))],
        out_specs=[
            pl.BlockSpec((1, tt, dc), lambda i, j: (i, j, 0)),
            pl.BlockSpec((1, HG_HEADS, HG_K, HG_V), lambda i, j: (i, 0, 0, 0)),
        ],
        out_shape=[
            jax.ShapeDtypeStruct((bsz, t, dc), BF16),
            jax.ShapeDtypeStruct((bsz, HG_HEADS, HG_K, HG_V), F32),
        ],
        scratch_shapes=[pltpu.VMEM((HG_HEADS, HG_V, HG_K), F32), pltpu.VMEM((HG_CHUNK, dc), F32),
                        pltpu.VMEM((2, tt, 4 * dc), BF16)],
        compiler_params=_cparams("arbitrary", "arbitrary"),
        name="hgrn_prompt",
    )(h3, h3, g_mix, w_in, lb, gn)


def _hgrn_sample_kernel(q_ref, f_ref, v_ref, g_ref, s_ref, lb_ref, gn_ref, *rest, nbk, layer):
    o_ref, s_out = rest[-2:]
    slot = 0
    if s_out.shape[0] > 1:
        slot = layer
        for l in range(s_out.shape[0]):
            if l != layer:
                s_out[l] = jnp.zeros(s_out.shape[1:], F32)
    for h in range(HG_HEADS):
        hs = slice(h * HG_K, (h + 1) * HG_K)
        q, logg, k = _hg_gates(q_ref[:, hs], f_ref[:, hs], lb_ref[:, hs])
        dec = jnp.exp(logg)
        v = v_ref[:, hs]
        rows = []
        for n in range(nbk):
            col = lambda x: jnp.broadcast_to(x[n:n + 1, :], (HG_K, HG_K)).T
            s_new = col(dec) * s_ref[0, n, h] + col(k) * v[n:n + 1, :]
            s_out[slot, n, h] = s_new
            rows.append(jnp.sum(col(q) * s_new, axis=0, keepdims=True))
        o = jnp.concatenate(rows, axis=0)
        o_ref[:, hs] = _hg_out(o, g_ref[:, hs], gn_ref[...]).astype(o_ref.dtype)


def _hgrn_sample(proj, s_all, layer, lb, gn, s_new_all=None):
    n = proj.shape[0]
    nbk = 8
    dc = HG_HEADS * HG_K
    n_layers = s_all.shape[0]
    kern = functools.partial(_hgrn_sample_kernel, nbk=nbk, layer=layer)
    blk = lambda c: pl.BlockSpec((nbk, dc), lambda i, c=c: (i, c))
    state_blk = pl.BlockSpec((1, nbk, HG_HEADS, HG_K, HG_V), lambda i: (layer, i, 0, 0, 0))
    in_specs = [blk(0), blk(1), blk(2), blk(3), state_blk,
                pl.BlockSpec((1, dc), lambda i: (0, 0)),
                pl.BlockSpec((1, HG_V), lambda i: (0, 0))]
    args = [proj, proj, proj, proj, s_all, lb, gn]
    if s_new_all is None:
        out_state, aliases = pl.BlockSpec((n_layers, nbk, HG_HEADS, HG_K, HG_V), lambda i: (0, i, 0, 0, 0)), {}
    else:
        in_specs.append(pl.BlockSpec(memory_space=pl.ANY))
        args.append(s_new_all)
        out_state, aliases = state_blk, {len(args) - 1: 1}
    return pl.pallas_call(
        kern,
        grid=(n // nbk,),
        in_specs=in_specs,
        out_specs=[pl.BlockSpec((nbk, dc), lambda i: (i, 0)), out_state],
        out_shape=[
            jax.ShapeDtypeStruct((n, dc), F32),
            jax.ShapeDtypeStruct(s_all.shape, F32),
        ],
        input_output_aliases=aliases,
        compiler_params=_cparams("parallel"),
        name="hgrn_sample",
    )(*args)


def _mix_ffn_kernel(h_ref, ya_ref, yb_ref, woa_ref, wob_ref, g_ref, wg_ref, wu_ref, wd_ref, o_ref,
                    h1_ref, xn_ref, acc_ref, *, precise):
    k = pl.program_id(1)

    @pl.when(k == 0)
    def _():
        h1 = (h_ref[...] + _pdot(ya_ref[...], woa_ref[...], precise)
              + _pdot(yb_ref[...], wob_ref[...], precise))
        h1_ref[...] = h1
        xn_ref[...] = _rms(h1, g_ref[...]).astype(xn_ref.dtype)
        acc_ref[...] = jnp.zeros(acc_ref.shape, F32)

    xn = xn_ref[...]
    a = _pdot(xn, wg_ref[...], precise)
    b = _pdot(xn, wu_ref[...], precise)
    acc_ref[...] += _pdot(_silu(a) * b, wd_ref[...], precise)

    @pl.when(k == pl.num_programs(1) - 1)
    def _():
        o_ref[...] = h1_ref[...] + acc_ref[...]


def _mix_ffn(h, ya, yb, woa, wob, g, wg, wu, wd):
    precise = wg.dtype == F32
    n, d = h.shape
    tm = min(512, n)
    tf = D_FF_PAD // 2
    row = lambda w: pl.BlockSpec((tm, w), lambda i, k: (i, 0))
    return pl.pallas_call(
        functools.partial(_mix_ffn_kernel, precise=precise),
        grid=(n // tm, D_FF_PAD // tf),
        in_specs=[
            row(d), row(D_A), row(D_B),
            pl.BlockSpec((D_A, d), lambda i, k: (0, 0)),
            pl.BlockSpec((D_B, d), lambda i, k: (0, 0)),
            pl.BlockSpec((1, d), lambda i, k: (0, 0)),
            pl.BlockSpec((d, tf), lambda i, k: (0, k)),
            pl.BlockSpec((d, tf), lambda i, k: (0, k)),
            pl.BlockSpec((tf, d), lambda i, k: (k, 0)),
        ],
        out_specs=row(d),
        out_shape=jax.ShapeDtypeStruct((n, d), F32),
        scratch_shapes=[pltpu.VMEM((tm, d), F32), pltpu.VMEM((tm, d), F32 if precise else BF16),
                        pltpu.VMEM((tm, d), F32)],
        compiler_params=_cparams("parallel", "arbitrary"),
        name="mix_ffn",
    )(h, ya, yb, woa, wob, g, wg, wu, wd)


def _mix_router_kernel(h_ref, y_ref, wo_ref, g_ref, rh_ref, rl_ref, *rest, precise):
    h1_ref, xn_ref, ids_ref, gts_ref = rest[-4:]
    h1 = h_ref[...] + _pdot(y_ref[...], wo_ref[...], precise)
    h1_ref[...] = h1
    xn = _rms(h1, g_ref[...])
    xn_ref[...] = xn
    xh = xn.astype(BF16)
    xl = (xn - xh.astype(F32)).astype(BF16)
    dot = lambda a, b: jnp.dot(a, b, preferred_element_type=F32)
    logits = dot(xh, rh_ref[...]) + (dot(xh, rl_ref[...]) + dot(xl, rh_ref[...]))
    lane = lax.broadcasted_iota(I32, logits.shape, 1)
    neg = jnp.float32(-jnp.inf)
    logits = jnp.where(lane < N_EXP, logits, neg)
    m1 = jnp.max(logits, axis=-1, keepdims=True)
    i1 = jnp.min(jnp.where(logits == m1, lane, LANES), axis=-1, keepdims=True)
    rest = jnp.where(lane == i1, neg, logits)
    m2 = jnp.max(rest, axis=-1, keepdims=True)
    i2 = jnp.min(jnp.where(rest == m2, lane, LANES), axis=-1, keepdims=True)
    e2 = jnp.exp(m2 - m1)
    g1 = 1.0 / (1.0 + e2)
    g2 = e2 / (1.0 + e2)
    ids = jnp.where(lane == 0, i1, jnp.where(lane == 1, i2, 0))
    ids_ref[...] = ids.T[0:8, :]
    gts_ref[...] = jnp.where(lane == 0, g1, jnp.where(lane == 1, g2, 0.0))


def _mix_router(h, y, wo, g, r_hi, r_lo, xn_all, row0):
    n, d = h.shape
    tm = min(1024, n)
    row = lambda w: pl.BlockSpec((tm, w), lambda i: (i, 0))
    return pl.pallas_call(
        functools.partial(_mix_router_kernel, precise=wo.dtype == F32),
        grid=(n // tm,),
        in_specs=[row(d), row(d),
                  pl.BlockSpec((d, d), lambda i: (0, 0)),
                  pl.BlockSpec((1, d), lambda i: (0, 0)),
                  pl.BlockSpec((d, LANES), lambda i: (0, 0)),
                  pl.BlockSpec((d, LANES), lambda i: (0, 0)),
                  pl.BlockSpec(memory_space=pl.ANY)],
        out_specs=[row(d), pl.BlockSpec((tm, d), lambda i: (i + row0 // tm, 0)),
                   pl.BlockSpec((8, tm), lambda i: (0, i)), row(LANES)],
        out_shape=[
            jax.ShapeDtypeStruct((n, d), F32),
            jax.ShapeDtypeStruct(xn_all.shape, F32),
            jax.ShapeDtypeStruct((8, n), I32),
            jax.ShapeDtypeStruct((n, LANES), F32),
        ],
        input_output_aliases={6: 1},
        compiler_params=_cparams("parallel"),
        name="mix_router",
    )(h, y, wo, g, r_hi, r_lo, xn_all)


def _wait_rows(src, dst, sem, copies):
    for _ in range(copies):
        pltpu.make_async_copy(src, dst, sem).wait()


def _dispatch_kernel(pos_ref, x_ref, xs_in, xs_hbm, sem, *, tt, n_tok):
    del xs_in
    i = pl.program_id(0)

    def body(r, c):
        t = i * tt + r
        for j in range(2):
            pltpu.make_async_copy(x_ref.at[pl.ds(r, 1), :], xs_hbm.at[pl.ds(pos_ref[j * n_tok + t], 1), :],
                                  sem.at[0]).start(priority=j)
        return c

    lax.fori_loop(0, tt, body, 0, unroll=8)
    _wait_rows(x_ref, xs_hbm.at[pl.ds(0, tt), :], sem.at[0], 2)


def _dispatch(x_all, pos, xs_zero):
    n, d = x_all.shape
    tt = 384 if n % 384 == 0 else LANES
    kern = functools.partial(_dispatch_kernel, tt=tt, n_tok=n)
    grid_spec = pltpu.PrefetchScalarGridSpec(
        num_scalar_prefetch=1,
        grid=(n // tt,),
        in_specs=[pl.BlockSpec((tt, d), lambda i, p: (i, 0)), pl.BlockSpec(memory_space=pl.ANY)],
        out_specs=pl.BlockSpec(memory_space=pl.ANY),
        scratch_shapes=[pltpu.SemaphoreType.DMA((1,))],
    )
    return pl.pallas_call(
        kern,
        grid_spec=grid_spec,
        out_shape=jax.ShapeDtypeStruct(xs_zero.shape, xs_zero.dtype),
        input_output_aliases={2: 0},
        compiler_params=_cparams("arbitrary"),
        name="moe_dispatch",
    )(pos, x_all, xs_zero)


def _expert_kernel(te_ref, nu_ref, xs_ref, wg_ref, wu_ref, wd_ref, ys_ref, xb, acc_ref):
    i = pl.program_id(0)
    k = pl.program_id(1)
    used = i < nu_ref[0]

    @pl.when(k == 0)
    def _():
        acc_ref[...] = jnp.zeros(acc_ref.shape, F32)
        xb[...] = xs_ref[...].astype(BF16)

    @pl.when(used)
    def _():
        x = xb[...]
        a = jnp.dot(x, wg_ref[0, 0].astype(BF16), preferred_element_type=F32)
        b = jnp.dot(x, wu_ref[0, 0].astype(BF16), preferred_element_type=F32)
        acc_ref[...] += jnp.dot((_silu(a) * b).astype(BF16), wd_ref[0, 0].astype(BF16),
                                preferred_element_type=F32)

    @pl.when(k == pl.num_programs(1) - 1)
    def _():
        ys_ref[...] = acc_ref[...]


def _experts(xs, tile_expert, n_used, wg, wu, wd, layer):
    n_rows = xs.shape[0]
    tm = MOE_TM
    nk = 7
    tk = D_EXP // nk

    def kk(i, k, nu):
        return jnp.where(i < nu[0], k, nk - 1)

    grid_spec = pltpu.PrefetchScalarGridSpec(
        num_scalar_prefetch=2,
        grid=(n_rows // tm, nk),
        in_specs=[
            pl.BlockSpec((tm, D_MODEL), lambda i, k, te, nu: (i, 0)),
            pl.BlockSpec((1, 1, D_MODEL, tk), lambda i, k, te, nu: (layer, te[i], 0, kk(i, k, nu))),
            pl.BlockSpec((1, 1, D_MODEL, tk), lambda i, k, te, nu: (layer, te[i], 0, kk(i, k, nu))),
            pl.BlockSpec((1, 1, tk, D_MODEL), lambda i, k, te, nu: (layer, te[i], kk(i, k, nu), 0)),
        ],
        out_specs=pl.BlockSpec((tm, D_MODEL), lambda i, k, te, nu: (i, 0)),
        scratch_shapes=[pltpu.VMEM((tm, D_MODEL), BF16), pltpu.VMEM((tm, D_MODEL), F32)],
    )
    return pl.pallas_call(
        _expert_kernel,
        grid_spec=grid_spec,
        out_shape=jax.ShapeDtypeStruct((n_rows, D_MODEL), F32),
        compiler_params=_cparams("arbitrary", "arbitrary"),
        name="moe_experts",
    )(tile_expert, n_used, xs, wg, wu, wd)


def _combine_kernel(pos_ref, h_ref, g_ref, ys_hbm, fin_ref, o_ref, buf, sem, *, tt, tok0, n_tok, final):
    i = pl.program_id(0)

    def issue(tile, slot):
        def body(r, c):
            t = tok0 + tile * tt + r
            for j in range(2):
                pltpu.make_async_copy(ys_hbm.at[pl.ds(pos_ref[j * n_tok + t], 1), :],
                                      buf.at[slot, j, pl.ds(r, 1), :], sem.at[slot]).start(priority=j)
            return c

        lax.fori_loop(0, tt, body, 0, unroll=8)

    @pl.when(i == 0)
    def _():
        issue(0, 0)

    @pl.when(i + 1 < pl.num_programs(0))
    def _():
        issue(i + 1, (i + 1) % 2)

    slot = i % 2
    _wait_rows(ys_hbm.at[pl.ds(0, tt), :], buf.at[slot, 0], sem.at[slot], 2)
    g = g_ref[...]
    out = h_ref[...] + (g[:, 0:1] * buf[slot, 0] + g[:, 1:2] * buf[slot, 1])
    o_ref[...] = _rms(out, fin_ref[...]) if final else out


def _combine(h1, gts, pos, ys, tok0, fin_g, final):
    n, d = h1.shape
    tt = min(512, n)
    kern = functools.partial(_combine_kernel, tt=tt, tok0=tok0, n_tok=pos.shape[0] // 2, final=final)
    grid_spec = pltpu.PrefetchScalarGridSpec(
        num_scalar_prefetch=1,
        grid=(n // tt,),
        in_specs=[pl.BlockSpec((tt, d), lambda i, p: (i, 0)),
                  pl.BlockSpec((tt, LANES), lambda i, p: (i, 0)),
                  pl.BlockSpec(memory_space=pl.ANY),
                  pl.BlockSpec((1, d), lambda i, p: (0, 0))],
        out_specs=pl.BlockSpec((tt, d), lambda i, p: (i, 0)),
        scratch_shapes=[pltpu.VMEM((2, 2, tt, d), F32), pltpu.SemaphoreType.DMA((2,))],
    )
    return pl.pallas_call(
        kern,
        grid_spec=grid_spec,
        out_shape=jax.ShapeDtypeStruct((n, d), F32),
        compiler_params=_cparams("arbitrary"),
        name="moe_combine",
    )(pos, h1, gts, ys, fin_g)


def _moe(h1_p, h1_s, x_all, ids_p, ids_s, gts_p, gts_s, xs_zero, wg, wu, wd, layer, fin_g, final):
    n_p = h1_p.shape[0]
    tm = MOE_TM
    n_tiles = xs_zero.shape[0] // tm
    e = jnp.concatenate([ids_p[:2], ids_s[:2]], axis=1)
    experts = jnp.arange(N_EXP, dtype=I32)[:, None]
    is_e = [e[j][None, :] == experts for j in range(2)]
    hit = is_e[0].astype(I32) + is_e[1].astype(I32)
    cum = jnp.cumsum(hit, axis=1)
    rank = cum - hit
    counts = cum[:, -1]
    padded = ((counts + tm - 1) // tm) * tm
    ends = jnp.cumsum(padded)
    offs = ends - padded
    row_of = offs[:, None] + rank
    pos = jnp.stack([jnp.sum(jnp.where(m, row_of, 0), axis=0) for m in is_e]).reshape(-1).astype(I32)
    n_used = (ends[-1] // tm).astype(I32).reshape(1)
    tile_start = jnp.arange(n_tiles, dtype=I32) * tm
    tile_expert = jnp.sum((ends[None, :] <= tile_start[:, None]).astype(I32), axis=1)
    tile_expert = jnp.minimum(tile_expert, N_EXP - 1).astype(I32)

    xs = _dispatch(x_all, pos, xs_zero)
    ys = _experts(xs, tile_expert, n_used, wg, wu, wd, layer)
    out_p = _combine(h1_p, gts_p, pos, ys, 0, fin_g, final)
    out_s = _combine(h1_s, gts_s, pos, ys, n_p, fin_g, final)
    return out_p, out_s


def _final_norm_kernel(x_ref, g_ref, o_ref):
    o_ref[...] = _rms(x_ref[...], g_ref[...])


def _final_norm(x, g):
    n, d = x.shape
    tm = min(1024, n)
    return pl.pallas_call(
        _final_norm_kernel,
        grid=(n // tm,),
        in_specs=[pl.BlockSpec((tm, d), lambda i: (i, 0)), pl.BlockSpec((1, d), lambda i: (0, 0))],
        out_specs=pl.BlockSpec((tm, d), lambda i: (i, 0)),
        out_shape=jax.ShapeDtypeStruct((n, d), F32),
        compiler_params=_cparams("parallel"),
        name="final_norm",
    )(x, g)


def kernel(x_prompt, x_sample, state_conv, state_ssm_re, state_ssm_im, state_hgrn, norm_mix, norm_ffn, norm_final, ab_w_in, ab_w_out, conv_w, conv_b, conv_ln_g, conv_ln_b, ssm_a_re, ssm_a_im, ssm_log_dt, ssm_b_re, ssm_b_im, ssm_c_re, ssm_c_im, ssm_d, ssm_glu_w, ssm_glu_b, hg_w_in, hg_lb_logits, hg_gnorm, hg_w_out, ffn_w_gate, ffn_w_up, ffn_w_down, moe_router, moe_w_gate, moe_w_up, moe_w_down):
    bsz, seq, d = x_prompt.shape
    n_s = x_sample.shape[0]
    depth = norm_mix.shape[0]
    hp = x_prompt.reshape(bsz * seq, d)
    hs = x_sample.reshape(n_s, d)

    lbs = jax.nn.softmax(hg_lb_logits.astype(F32), axis=0)
    lbs = jnp.cumsum(lbs, axis=0) - lbs[0:1]

    conv_p, re_p, im_p, hg_p = [], [], [], []
    conv_s, re_s, im_s, hg_s = [], [], [], None
    zero_state = jnp.zeros((bsz, S5_STATE), F32)
    x_all = jnp.zeros((bsz * seq + n_s, d), F32)
    moe_wg, moe_wu, moe_wd = moe_w_gate.astype(F32), moe_w_up.astype(F32), moe_w_down.astype(F32)
    row = lambda v: v.reshape(1, -1).astype(F32)
    g_fin = row(norm_final)

    for l in range(depth):
        j = l // 2
        g_mix, g_ffn = row(norm_mix[l]), row(norm_ffn[l])
        if l % 2 == 0:
            w_in = ab_w_in[j].astype(BF16)
            woa = ab_w_out[j, :D_A].astype(BF16)
            wob = ab_w_out[j, D_A:].astype(BF16)
            cw, cb, clg, clb = conv_w[j].astype(F32), row(conv_b[j]), row(conv_ln_g[j]), row(conv_ln_b[j])
            abar_re, abar_im, bcat, ccat = _s5_params(
                ssm_a_re[j].astype(F32), ssm_a_im[j].astype(F32), ssm_log_dt[j].astype(F32),
                ssm_b_re[j].astype(F32), ssm_b_im[j].astype(F32), ssm_c_re[j].astype(F32), ssm_c_im[j].astype(F32))
            sd, sgw, sgb = row(ssm_d[j]), ssm_glu_w[j].astype(BF16), row(ssm_glu_b[j])
            pad = ((0, 0), (0, D_FF_PAD - D_FF))
            wg = jnp.pad(ffn_w_gate[j], pad).astype(BF16)
            wu = jnp.pad(ffn_w_up[j], pad).astype(BF16)
            wd = jnp.pad(ffn_w_down[j], (pad[1], pad[0])).astype(BF16)

            proj = _norm_matmul(hp, g_mix, w_in, tn=w_in.shape[1], out_dtype=BF16)
            proj3 = proj.reshape(bsz, seq, -1)
            ya, cb_new = _conv_prompt(proj3, cw, cb, clg, clb)
            yb, hr, hi = _s5(proj3, zero_state, zero_state, abar_re, abar_im, bcat, ccat, sd, sgw, sgb,
                             tt=min(128, seq), seq_major=True)
            hp = _mix_ffn(hp, ya.reshape(bsz * seq, D_A), yb.reshape(bsz * seq, D_B), woa, wob, g_ffn,
                          wg, wu, wd)
            conv_p.append(cb_new)
            re_p.append(hr.reshape(bsz, S5_G, S5_P))
            im_p.append(hi.reshape(bsz, S5_G, S5_P))

            proj = _norm_matmul(hs, g_mix, ab_w_in[j].astype(F32), tn=w_in.shape[1])
            ya, nb_t = _conv_sample(proj, jnp.transpose(state_conv[j].astype(F32), (1, 0, 2)), cw, cb, clg, clb)
            yb_t, hr, hi = _s5(proj[:, 2 * D_A:].reshape(1, n_s, D_B),
                               state_ssm_re[j].astype(F32).reshape(n_s, S5_STATE),
                               state_ssm_im[j].astype(F32).reshape(n_s, S5_STATE),
                               abar_re, abar_im, bcat, ccat, sd, sgw, sgb, tt=1, seq_major=False)
            hs = _mix_ffn(hs, ya, yb_t.reshape(n_s, D_B), ab_w_out[j, :D_A].astype(F32),
                          ab_w_out[j, D_A:].astype(F32), g_ffn, jnp.pad(ffn_w_gate[j].astype(F32), pad),
                          jnp.pad(ffn_w_up[j].astype(F32), pad),
                          jnp.pad(ffn_w_down[j].astype(F32), (pad[1], pad[0])))
            conv_s.append(jnp.transpose(nb_t, (1, 0, 2)))
            re_s.append(hr.reshape(n_s, S5_G, S5_P))
            im_s.append(hi.reshape(n_s, S5_G, S5_P))
        else:
            w_in = hg_w_in[j].astype(BF16)
            wo = hg_w_out[j].astype(BF16)
            lb, gn = row(lbs[j]), row(hg_gnorm[j])
            router_pad = jnp.pad(moe_router[j].astype(F32), ((0, 0), (0, LANES - N_EXP)))
            r_hi = router_pad.astype(BF16)
            r_lo = (router_pad - r_hi.astype(F32)).astype(BF16)

            o, s_new = _hgrn_prompt(hp.reshape(bsz, seq, d), g_mix, w_in, lb, gn)
            h1_p, x_all, ids_p, gts_p = _mix_router(hp, o.reshape(bsz * seq, d), wo, g_ffn, r_hi, r_lo,
                                                    x_all, 0)
            hg_p.append(s_new)

            proj = _norm_matmul(hs, g_mix, hg_w_in[j].astype(F32), tn=1024)
            o, hg_s = _hgrn_sample(proj, state_hgrn.astype(F32), j, lb, gn, hg_s)
            h1_s, x_all, ids_s, gts_s = _mix_router(hs, o, hg_w_out[j].astype(F32), g_ffn, r_hi, r_lo, x_all,
                                                    bsz * seq)

            n_rows = (2 * x_all.shape[0] + N_EXP * (MOE_TM - 1)) // MOE_TM * MOE_TM
            hp, hs = _moe(h1_p, h1_s, x_all, ids_p, ids_s, gts_p, gts_s, jnp.zeros((n_rows, d), F32),
                          moe_wg, moe_wu, moe_wd, j, g_fin, l == depth - 1)

    if depth % 2 == 1:
        hp, hs = _final_norm(hp, g_fin), _final_norm(hs, g_fin)
    y_prompt = hp.reshape(bsz, seq, d)
    y_sample = hs.reshape(n_s, 1, d)
    return (y_prompt, y_sample, jnp.stack(conv_p), jnp.stack(re_p), jnp.stack(im_p), jnp.stack(hg_p),
            jnp.stack(conv_s), jnp.stack(re_s), jnp.stack(im_s), hg_s)
```
